```python
import jax, jax.numpy as jnp
from jax import lax
import numpy as np

D_MODEL = 1024
BATCH = 8
SEQ = 2048
DEPTH = 4

CTX_LEN = 256
GRID_W = 64
N_MIXERS = 4
BRANCH_W = D_MODEL // 2
MIX_W = N_MIXERS * BRANCH_W
HEAD_DIM = 64
N_HEADS_A = BRANCH_W // HEAD_DIM
A_CONV = 4
RG_C = 8.0
B_CONV = 3
C_CONV = 31
POOL_WINDOWS = (2, 4, 8, 16)
POOL_GROUP = BRANCH_W // len(POOL_WINDOWS)
N_IN_SLOTS = 11
IN_W = N_IN_SLOTS * BRANCH_W
RMS_EPS = 1e-6
LN_EPS = 1e-5

kernel_name = "hybrid_parallel_group_flow_backbone"


def rms_norm(x, g):
    xf = x.astype(jnp.float32)
    y = xf * lax.rsqrt(jnp.mean(xf * xf, axis=-1, keepdims=True) + RMS_EPS)
    return (y * g.astype(jnp.float32)).astype(x.dtype)


def layer_norm(x, g, b):
    xf = x.astype(jnp.float32)
    mu = jnp.mean(xf, axis=-1, keepdims=True)
    xc = xf - mu
    var = jnp.mean(xc * xc, axis=-1, keepdims=True)
    y = xc * lax.rsqrt(var + LN_EPS) * g.astype(jnp.float32) + b.astype(jnp.float32)
    return y.astype(x.dtype)


def depthwise_conv(x, w, pad_left, pad_right):
    k = w[:, None, :].astype(x.dtype)
    return lax.conv_general_dilated(
        x, k, window_strides=(1,), padding=[(pad_left, pad_right)],
        dimension_numbers=('NWC', 'WIO', 'NWC'), feature_group_count=x.shape[-1])


def _rev_bwd(z):
    return jnp.stack([z[0], jnp.flip(z[1], axis=1)])


def rglru(v, lp, h0):
    bsz, t, _ = v.shape
    xc = jnp.stack([depthwise_conv(v, lp['a_conv'][0], A_CONV - 1, 0),
                    depthwise_conv(v, lp['a_conv'][1], 0, A_CONV - 1)])
    xh = xc.reshape(2, bsz, t, N_HEADS_A, HEAD_DIM)
    r = jax.nn.sigmoid(jnp.einsum('nbthi,nhij->nbthj', xh, lp['a_wr']).reshape(2, bsz, t, BRANCH_W)
                       + lp['a_br'][:, None, None, :])
    ig = jax.nn.sigmoid(jnp.einsum('nbthi,nhij->nbthj', xh, lp['a_wi']).reshape(2, bsz, t, BRANCH_W)
                        + lp['a_bi'][:, None, None, :])
    log_a = -RG_C * r.astype(jnp.float32) * jax.nn.softplus(-lp['a_lam'].astype(jnp.float32))[:, None, None, :]
    a = jnp.exp(log_a)
    b = jnp.sqrt(-jnp.expm1(2.0 * log_a)) * (ig * xc).astype(jnp.float32)
    a = _rev_bwd(a)
    b = _rev_bwd(b)

    def step(h, ab):
        at, bt = ab
        h = at * h + bt
        return h, h

    h_final, ys = lax.scan(step, h0, (jnp.moveaxis(a, 2, 0), jnp.moveaxis(b, 2, 0)))
    ys = _rev_bwd(jnp.moveaxis(ys, 0, 2))
    return (ys[0] + ys[1]).astype(v.dtype), h_final


def pool_grid(u):
    bsz, s, ch = u.shape
    rows = s // GRID_W
    g = u.reshape(bsz, rows, GRID_W, ch).astype(jnp.float32)
    P = jnp.pad(jnp.cumsum(jnp.cumsum(g, axis=1), axis=2), ((0, 0), (1, 0), (1, 0), (0, 0)))
    r = jnp.arange(rows)
    cidx = jnp.arange(GRID_W)
    outs = []
    for gi, w in enumerate(POOL_WINDOWS):
        Pg = P[..., gi * POOL_GROUP:(gi + 1) * POOL_GROUP]
        r0 = jnp.clip(r - w // 2, 0, rows)
        r1 = jnp.clip(r + w // 2, 0, rows)
        c0 = jnp.clip(cidx - w // 2, 0, GRID_W)
        c1 = jnp.clip(cidx + w // 2, 0, GRID_W)
        Pr1 = Pg[:, r1]
        Pr0 = Pg[:, r0]
        total = Pr1[:, :, c1] - Pr1[:, :, c0] - Pr0[:, :, c1] + Pr0[:, :, c0]
        cnt = ((r1 - r0)[:, None] * (c1 - c0)[None, :]).astype(jnp.float32)
        outs.append(total / cnt[None, :, :, None])
    mean = jnp.concatenate(outs, axis=-1).reshape(bsz, s, ch)
    return mean.astype(u.dtype) - u


def pool_seq(u):
    bsz, t, ch = u.shape
    P = jnp.pad(jnp.cumsum(u.astype(jnp.float32), axis=1), ((0, 0), (1, 0), (0, 0)))
    pos = jnp.arange(t)
    outs = []
    for gi, w in enumerate(POOL_WINDOWS):
        Pg = P[..., gi * POOL_GROUP:(gi + 1) * POOL_GROUP]
        lo = jnp.clip(pos - w // 2, 0, t)
        hi = jnp.clip(pos + w // 2, 0, t)
        outs.append((Pg[:, hi] - Pg[:, lo]) / (hi - lo).astype(jnp.float32)[None, :, None])
    mean = jnp.concatenate(outs, axis=-1)
    return mean.astype(u.dtype) - u


def mix(h, lp, h0, pool_fn):
    proj = jnp.einsum('btd,de->bte', h, lp['w_in'])
    (va, za, bb, bc, bv, zb, ca, cg, zc, dv, zd) = jnp.split(proj, N_IN_SLOTS, axis=-1)
    ya, h_final = rglru(va, lp, h0)
    yb = bb * depthwise_conv(bc * bv, lp['b_conv'], B_CONV // 2, B_CONV // 2)
    u = ca * jax.nn.sigmoid(cg)
    u = depthwise_conv(u, lp['c_conv'], C_CONV // 2, C_CONV // 2)
    u = layer_norm(u, lp['c_ln_g'], lp['c_ln_b'])
    yc = jnp.einsum('btc,ce->bte', jax.nn.silu(u), lp['c_pw']) + lp['c_pw_b']
    p = pool_fn(dv)
    bsz, t, _ = p.shape
    pg = p.reshape(bsz, t, len(POOL_WINDOWS), POOL_GROUP)
    yd = (jnp.einsum('btgi,gij->btgj', pg, lp['d_w']) + lp['d_b']).reshape(bsz, t, BRANCH_W) * lp['d_scale']
    cat = jnp.concatenate([ya * jax.nn.silu(za), yb * jax.nn.silu(zb),
                           yc * jax.nn.silu(zc), yd * jax.nn.silu(zd)], axis=-1)
    return jnp.einsum('bte,ed->btd', cat, lp['w_out']), h_final


def setup_inputs(seed: int = 0) -> dict:
    key = jax.random.key(seed)
    ks = jax.random.split(key, 26)
    f32 = jnp.float32
    nrm = lambda k, shape, s: jax.random.normal(k, shape, f32) * s
    u = jax.random.uniform(ks[12], (DEPTH, 2, BRANCH_W), f32, minval=0.9, maxval=0.999)
    pa = u ** (1.0 / RG_C)
    a_lam = jnp.log(pa) - jnp.log1p(-pa)
    return {
        "x": nrm(ks[0], (BATCH, SEQ, D_MODEL), 1.0),
        "c": nrm(ks[1], (BATCH, D_MODEL), 1.0),
        "ctx": nrm(ks[2], (BATCH, CTX_LEN, D_MODEL), 1.0),
        "c_ctx": nrm(ks[3], (D_MODEL,), 1.0),
        "mod_w": nrm(ks[4], (DEPTH, D_MODEL, 3 * D_MODEL), 0.5 * D_MODEL ** -0.5),
        "mod_b": nrm(ks[5], (DEPTH, 3 * D_MODEL), 0.02),
        "norm_g": 1.0 + nrm(ks[6], (DEPTH, D_MODEL), 0.05),
        "w_in": nrm(ks[7], (DEPTH, D_MODEL, IN_W), D_MODEL ** -0.5),
        "w_out": nrm(ks[8], (DEPTH, MIX_W, D_MODEL), MIX_W ** -0.5),
        "a_conv": nrm(ks[9], (DEPTH, 2, A_CONV, BRANCH_W), A_CONV ** -0.5),
        "a_wr": nrm(ks[10], (DEPTH, 2, N_HEADS_A, HEAD_DIM, HEAD_DIM), HEAD_DIM ** -0.5),
        "a_br": nrm(ks[11], (DEPTH, 2, BRANCH_W), 0.1),
        "a_wi": nrm(ks[13], (DEPTH, 2, N_HEADS_A, HEAD_DIM, HEAD_DIM), HEAD_DIM ** -0.5),
        "a_bi": nrm(ks[14], (DEPTH, 2, BRANCH_W), 0.1),
        "a_lam": a_lam,
        "b_conv": nrm(ks[15], (DEPTH, B_CONV, BRANCH_W), B_CONV ** -0.5),
        "c_conv": nrm(ks[16], (DEPTH, C_CONV, BRANCH_W), C_CONV ** -0.5),
        "c_ln_g": 1.0 + nrm(ks[17], (DEPTH, BRANCH_W), 0.05),
        "c_ln_b": nrm(ks[18], (DEPTH, BRANCH_W), 0.02),
        "c_pw": nrm(ks[19], (DEPTH, BRANCH_W, BRANCH_W), BRANCH_W ** -0.5),
        "c_pw_b": nrm(ks[20], (DEPTH, BRANCH_W), 0.02),
        "d_w": nrm(ks[21], (DEPTH, len(POOL_WINDOWS), POOL_GROUP, POOL_GROUP), POOL_GROUP ** -0.5),
        "d_b": nrm(ks[22], (DEPTH, len(POOL_WINDOWS), POOL_GROUP), 0.02),
        "d_scale": 1.0 + nrm(ks[23], (DEPTH, BRANCH_W), 0.1),
        "final_g": 1.0 + nrm(ks[24], (D_MODEL,), 0.05),
    }


def reference(x, c, ctx, c_ctx, mod_w, mod_b, norm_g, w_in, w_out, a_conv, a_wr, a_br, a_wi, a_bi,
              a_lam, b_conv, c_conv, c_ln_g, c_ln_b, c_pw, c_pw_b, d_w, d_b, d_scale, final_g):
    bsz = x.shape[0]
    for l in range(DEPTH):
        last = l == DEPTH - 1
        lp = dict(w_in=w_in[l], w_out=w_out[l], a_conv=a_conv[l], a_wr=a_wr[l], a_br=a_br[l],
                  a_wi=a_wi[l], a_bi=a_bi[l], a_lam=a_lam[l], b_conv=b_conv[l], c_conv=c_conv[l],
                  c_ln_g=c_ln_g[l], c_ln_b=c_ln_b[l], c_pw=c_pw[l], c_pw_b=c_pw_b[l],
                  d_w=d_w[l], d_b=d_b[l], d_scale=d_scale[l])
        mod = jax.nn.silu(c) @ mod_w[l] + mod_b[l]
        shift, scale, gate = jnp.split(mod[:, None, :], 3, axis=-1)
        mod_k = jax.nn.silu(c_ctx) @ mod_w[l] + mod_b[l]
        shift_k, scale_k, gate_k = jnp.split(mod_k, 3)
        hk = rms_norm(ctx, norm_g[l]) * (1.0 + scale_k) + shift_k
        h = rms_norm(x, norm_g[l]) * (1.0 + scale) + shift
        h0 = jnp.zeros((2, bsz, BRANCH_W), jnp.float32)
        if last:
            vk = jnp.einsum('btd,de->bte', hk, lp['w_in'][:, :BRANCH_W])
            _, state_k = rglru(vk, lp, h0)
        else:
            yk, state_k = mix(hk, lp, h0, pool_seq)
            ctx = ctx + gate_k * yk
        y, _ = mix(h, lp, state_k, pool_grid)
        x = x + gate * y
    return rms_norm(x, final_g)
```

```python
import functools

import jax
import jax.numpy as jnp
from jax import lax
from jax.experimental import pallas as pl
from jax.experimental.pallas import tpu as pltpu

F32 = jnp.float32
BF16 = jnp.bfloat16

SUBLANES = 8
LANES = 128
BRANCH_W = 512
N_HEADS_A = 8
A_CONV = 4
B_CONV = 3
C_CONV = 31
RG_C = 8.0
GRID_W = 64
POOL_WINDOWS = (2, 4, 8, 16)
RMS_EPS = 1e-6
LN_EPS = 1e-5

A_HALO_TOK = 4
MIX_HALO_TOK = 16
CONV_TOK_BLOCK = 8
VMEM_LIMIT = 60 * 1024 * 1024


def _sigmoid(v):
    return 0.5 * (1.0 + jnp.tanh(0.5 * v))


def _silu(v):
    return v * _sigmoid(v)


def _norm_mod(xv, g, sc, sh):
    rows, d = xv.shape
    ms = jnp.mean(xv * xv, axis=-1, keepdims=True)
    y = xv * lax.rsqrt(ms + RMS_EPS) * g
    y3 = y.reshape(rows // SUBLANES, SUBLANES, d)
    hmod = y3 * (1.0 + sc)[None] + sh[None]
    return hmod.reshape(rows, d)


def _mod_body(c_ref, w_ref, b_ref, o_ref):
    s = _silu(c_ref[...])
    o_ref[0] = jnp.dot(s, w_ref[0], preferred_element_type=F32,
                       precision=lax.Precision.HIGHEST) + b_ref[0]


def _modulation(cc, mod_w, mod_b):
    depth, d, d3 = mod_w.shape
    nj = d3 // d
    rows = cc.shape[0]
    return pl.pallas_call(
        _mod_body,
        grid=(depth, nj),
        in_specs=[pl.BlockSpec((rows, d), lambda l, j: (0, 0)),
                  pl.BlockSpec((1, d, d), lambda l, j: (l, 0, j)),
                  pl.BlockSpec((1, 1, d), lambda l, j: (l, 0, j))],
        out_specs=pl.BlockSpec((1, rows, d), lambda l, j: (l, 0, j)),
        out_shape=jax.ShapeDtypeStruct((depth, rows, d3), F32),
        compiler_params=pltpu.CompilerParams(
            dimension_semantics=("arbitrary", "arbitrary"), vmem_limit_bytes=VMEM_LIMIT),
        name="modulation",
    )(cc, mod_w, mod_b.reshape(depth, 1, d3))


def _rglru_gates(xc, wg, bg, lam):
    w = xc.shape[1]
    gates = jnp.dot(xc.astype(BF16), wg, preferred_element_type=F32) + bg
    r = _sigmoid(gates[:, :w])
    ig = _sigmoid(gates[:, w:])
    nl = -lam
    softplus = jnp.maximum(nl, 0.0) + jnp.log1p(jnp.exp(-jnp.abs(nl)))
    log_a = (-RG_C * softplus) * r
    a = jnp.exp(log_a)
    b = jnp.sqrt(-jnp.tanh(log_a) * (1.0 + a * a)) * (ig * xc)
    return a, b


def _rglru_body(ta, xf_ref, xfp_ref, xb_ref, xbn_ref, sc_ref, sh_ref, g_ref, wv_ref, wd_ref,
                aconv_ref, wg_ref, bg_ref, lam_ref, h0_ref,
                yf_ref, yb_ref, dv_ref, hfin_ref, st_ref, a_scr, b_scr):
    j = pl.program_id(0)
    rows = ta * SUBLANES
    halo = A_HALO_TOK * SUBLANES
    w = BRANCH_W

    @pl.when(j == 0)
    def _():
        st_ref[...] = h0_ref[...]

    g = g_ref[...]
    sc = sc_ref[...]
    sh = sh_ref[...]
    inner = j > 0

    hf = _norm_mod(jnp.concatenate([xfp_ref[...], xf_ref[...]], axis=0), g, sc, sh).astype(BF16)
    va = jnp.dot(hf, wv_ref[...], preferred_element_type=F32)
    dv_ref[...] = jnp.dot(hf[halo:], wd_ref[...], preferred_element_type=F32)
    va_h = jnp.where(inner, va[:halo], 0.0)
    va = jnp.concatenate([va_h, va[halo:]], axis=0)
    xc = None
    for k in range(A_CONV):
        off = (A_HALO_TOK - (A_CONV - 1) + k) * SUBLANES
        term = aconv_ref[0, k:k + 1, :] * va[off:off + rows]
        xc = term if xc is None else xc + term
    a, b = _rglru_gates(xc, wg_ref[0], bg_ref[0], lam_ref[0])
    a_scr[0] = a
    b_scr[0] = b

    hb = _norm_mod(jnp.concatenate([xb_ref[...], xbn_ref[...]], axis=0), g, sc, sh).astype(BF16)
    vb = jnp.dot(hb, wv_ref[...], preferred_element_type=F32)
    vb_h = jnp.where(inner, vb[rows:], 0.0)
    vb = jnp.concatenate([vb[:rows], vb_h], axis=0)
    xc = None
    for k in range(A_CONV):
        off = k * SUBLANES
        term = aconv_ref[1, k:k + 1, :] * vb[off:off + rows]
        xc = term if xc is None else xc + term
    a, b = _rglru_gates(xc, wg_ref[1], bg_ref[1], lam_ref[1])
    a_scr[1] = a
    b_scr[1] = b

    def step(i, carry):
        h_f, h_b = carry
        rf = pl.multiple_of(i * SUBLANES, SUBLANES)
        rb = pl.multiple_of((ta - 1 - i) * SUBLANES, SUBLANES)
        h_f = a_scr[0, pl.ds(rf, SUBLANES), :] * h_f + b_scr[0, pl.ds(rf, SUBLANES), :]
        yf_ref[pl.ds(rf, SUBLANES), :] = h_f
        h_b = a_scr[1, pl.ds(rb, SUBLANES), :] * h_b + b_scr[1, pl.ds(rb, SUBLANES), :]
        yb_ref[pl.ds(rb, SUBLANES), :] = h_b
        return h_f, h_b

    h_f, h_b = lax.fori_loop(0, ta, step, (st_ref[0], st_ref[1]), unroll=4)
    st_ref[0] = h_f
    st_ref[1] = h_b
    hfin_ref[0] = h_f
    hfin_ref[1] = h_b


def _rglru_call(xt, sc, sh, g, wv, wd, aconv, wg, bg, lam, h0, ta):
    n_rows, d = xt.shape
    rows = ta * SUBLANES
    halo = A_HALO_TOK * SUBLANES
    n = n_rows // rows
    per = rows // halo
    last_halo = n_rows // halo - 1
    w = BRANCH_W
    const2 = lambda j: (0, 0)
    const3 = lambda j: (0, 0, 0)
    in_specs = [
        pl.BlockSpec((rows, d), lambda j: (j, 0)),
        pl.BlockSpec((halo, d), lambda j: (jnp.maximum(j * per - 1, 0), 0)),
        pl.BlockSpec((rows, d), lambda j: (n - 1 - j, 0)),
        pl.BlockSpec((halo, d), lambda j: (jnp.minimum((n - j) * per, last_halo), 0)),
        pl.BlockSpec((SUBLANES, d), const2),
        pl.BlockSpec((SUBLANES, d), const2),
        pl.BlockSpec((1, d), const2),
        pl.BlockSpec((d, w), const2),
        pl.BlockSpec((d, w), const2),
        pl.BlockSpec((2, A_CONV, w), const3),
        pl.BlockSpec((2, w, 2 * w), const3),
        pl.BlockSpec((2, 1, 2 * w), const3),
        pl.BlockSpec((2, 1, w), const3),
        pl.BlockSpec((2, SUBLANES, w), const3),
    ]
    out_specs = [
        pl.BlockSpec((rows, w), lambda j: (j, 0)),
        pl.BlockSpec((rows, w), lambda j: (n - 1 - j, 0)),
        pl.BlockSpec((rows, w), lambda j: (j, 0)),
        pl.BlockSpec((2, SUBLANES, w), const3),
    ]
    out_shape = [jax.ShapeDtypeStruct((n_rows, w), F32)] * 3 + [
        jax.ShapeDtypeStruct((2, SUBLANES, w), F32)]
    return pl.pallas_call(
        functools.partial(_rglru_body, ta),
        grid=(n,),
        in_specs=in_specs,
        out_specs=out_specs,
        out_shape=out_shape,
        scratch_shapes=[pltpu.VMEM((2, SUBLANES, w), F32),
                        pltpu.VMEM((2, rows, w), F32),
                        pltpu.VMEM((2, rows, w), F32)],
        compiler_params=pltpu.CompilerParams(
            dimension_semantics=("arbitrary",), vmem_limit_bytes=VMEM_LIMIT),
        name="rglru",
    )(xt, xt, xt, xt, sc, sh, g, wv, wd, aconv, wg, bg, lam, h0)


def _window_sums(padded, w):
    f = padded
    k = 1
    while k < w:
        f = f[:-k] + f[k:]
        k *= 2
    return f


def _window_count(pos, half, size):
    return jnp.minimum(pos + half, size) - jnp.maximum(pos - half, 0)


def _pool_grid_one(v_ref, o_ref, p_scr, w):
    grid_rows = v_ref.shape[0]
    half = w // 2
    tile = v_ref.shape[2:]
    zpad = jnp.zeros((half,) + tile, F32)
    p_scr[0:half] = jnp.zeros((half, GRID_W) + tile, F32)
    p_scr[half + grid_rows:2 * half + grid_rows] = jnp.zeros((half, GRID_W) + tile, F32)

    def row_body(r, carry):
        f = _window_sums(jnp.concatenate([zpad, v_ref[r], zpad], axis=0), w)
        p_scr[half + r] = f[:GRID_W]
        return carry

    lax.fori_loop(0, grid_rows, row_body, 0)

    rpos = lax.broadcasted_iota(jnp.int32, (grid_rows,) + tile, 0)
    rcnt = _window_count(rpos, half, grid_rows)

    def col_body(c, carry):
        f = _window_sums(p_scr[pl.ds(0, grid_rows + 2 * half), c], w)
        cnt = (rcnt * _window_count(c, half, GRID_W)).astype(F32)
        o_ref[:, c] = f[:grid_rows] / cnt - v_ref[:, c]
        return carry

    lax.fori_loop(0, GRID_W, col_body, 0)


def _pool_grid_body(v_ref, o_ref, p_scr):
    gidx = pl.program_id(0)
    for gi, w in enumerate(POOL_WINDOWS):
        @pl.when(gidx == gi)
        def _(w=w):
            _pool_grid_one(v_ref, o_ref, p_scr, w)


def _pool_grid_call(dv):
    n_rows, w = dv.shape
    grid_rows = n_rows // (SUBLANES * GRID_W)
    v4 = dv.reshape(grid_rows, GRID_W, SUBLANES, w)
    blk = (grid_rows, GRID_W, SUBLANES, LANES)
    spec = pl.BlockSpec(blk, lambda gidx: (0, 0, 0, gidx))
    out = pl.pallas_call(
        _pool_grid_body,
        grid=(len(POOL_WINDOWS),),
        in_specs=[spec],
        out_specs=spec,
        out_shape=jax.ShapeDtypeStruct(v4.shape, F32),
        scratch_shapes=[pltpu.VMEM((grid_rows + max(POOL_WINDOWS), GRID_W, SUBLANES, LANES), F32)],
        compiler_params=pltpu.CompilerParams(
            dimension_semantics=("arbitrary",), vmem_limit_bytes=VMEM_LIMIT),
        name="pool_grid",
    )(v4)
    return out.reshape(n_rows, w)


def _pool_seq_body(v_ref, o_ref):
    gidx = pl.program_id(0)
    t = v_ref.shape[0]
    tile = v_ref.shape[1:]
    for gi, w in enumerate(POOL_WINDOWS):
        @pl.when(gidx == gi)
        def _(w=w):
            half = w // 2
            v = v_ref[...]
            zpad = jnp.zeros((half,) + tile, F32)
            f = _window_sums(jnp.concatenate([zpad, v, zpad], axis=0), w)
            pos = lax.broadcasted_iota(jnp.int32, (t,) + tile, 0)
            cnt = _window_count(pos, half, t).astype(F32)
            o_ref[...] = f[:t] / cnt - v


def _pool_seq_call(dv):
    n_rows, w = dv.shape
    t = n_rows // SUBLANES
    v3 = dv.reshape(t, SUBLANES, w)
    spec = pl.BlockSpec((t, SUBLANES, LANES), lambda gidx: (0, 0, gidx))
    out = pl.pallas_call(
        _pool_seq_body,
        grid=(len(POOL_WINDOWS),),
        in_specs=[spec],
        out_specs=spec,
        out_shape=jax.ShapeDtypeStruct(v3.shape, F32),
        compiler_params=pltpu.CompilerParams(
            dimension_semantics=("arbitrary",), vmem_limit_bytes=VMEM_LIMIT),
        name="pool_seq",
    )(v3)
    return out.reshape(n_rows, w)


def _mix_body(tb, final_norm, x_ref, xp_ref, xn_ref, sc_ref, sh_ref, gt_ref, g_ref, wc_ref, wo_ref,
              bconv_ref, cconv_ref, lng_ref, lnb_ref, cpw_ref, cpwb_ref, dw_ref, db_ref, dsc_ref,
              wout_ref, yf_ref, yb_ref, p_ref, fg_ref, o_ref, ub_scr, uc_scr, cv_scr):
    j = pl.program_id(0)
    n = pl.num_programs(0)
    rows = tb * SUBLANES
    halo = MIX_HALO_TOK * SUBLANES
    w = BRANCH_W
    d = x_ref.shape[1]

    g = g_ref[...]
    sc = sc_ref[...]
    sh = sh_ref[...]
    x_main = x_ref[...]
    h_main = _norm_mod(x_main, g, sc, sh).astype(BF16)
    h_prev = _norm_mod(xp_ref[...], g, sc, sh).astype(BF16)
    h_next = _norm_mod(xn_ref[...], g, sc, sh).astype(BF16)
    h_ext = jnp.concatenate([h_prev, h_main, h_next], axis=0)

    pc = jnp.dot(h_ext, wc_ref[...], preferred_element_type=F32)
    ub = pc[:, 0:w] * pc[:, w:2 * w]
    uc = pc[:, 2 * w:3 * w] * _sigmoid(pc[:, 3 * w:4 * w])
    has_prev = j > 0
    has_next = j < n - 1
    ub_scr[0:halo] = jnp.where(has_prev, ub[0:halo], 0.0)
    ub_scr[halo:halo + rows] = ub[halo:halo + rows]
    ub_scr[halo + rows:] = jnp.where(has_next, ub[halo + rows:], 0.0)
    uc_scr[0:halo] = jnp.where(has_prev, uc[0:halo], 0.0)
    uc_scr[halo:halo + rows] = uc[halo:halo + rows]
    uc_scr[halo + rows:] = jnp.where(has_next, uc[halo + rows:], 0.0)

    po = jnp.dot(h_main, wo_ref[...], preferred_element_type=F32)
    za = po[:, 0:w]
    bb = po[:, w:2 * w]
    zb = po[:, 2 * w:3 * w]
    zc = po[:, 3 * w:4 * w]
    zd = po[:, 4 * w:5 * w]

    cat_a = ((yf_ref[...] + yb_ref[...]) * _silu(za)).astype(BF16)
    out = jnp.dot(cat_a, wout_ref[0:w, :], preferred_element_type=F32)

    conv_b = None
    for k in range(B_CONV):
        off = (MIX_HALO_TOK + k - B_CONV // 2) * SUBLANES
        term = bconv_ref[k:k + 1, :] * ub_scr[off:off + rows, :]
        conv_b = term if conv_b is None else conv_b + term
    cat_b = (bb * conv_b * _silu(zb)).astype(BF16)
    out = out + jnp.dot(cat_b, wout_ref[w:2 * w, :], preferred_element_type=F32)

    span = CONV_TOK_BLOCK + C_CONV - 1
    first = MIX_HALO_TOK - C_CONV // 2
    blk_rows = CONV_TOK_BLOCK * SUBLANES
    for lb in range(w // LANES):
        lanes = slice(lb * LANES, (lb + 1) * LANES)

        def conv_block(i, carry, lanes=lanes):
            base = pl.multiple_of(i * blk_rows, blk_rows)
            ins = [uc_scr[pl.ds(base + (first + m) * SUBLANES, SUBLANES), lanes] for m in range(span)]
            accs = [None] * CONV_TOK_BLOCK
            for k in range(C_CONV):
                wk = cconv_ref[k, :, lanes]
                for q in range(CONV_TOK_BLOCK):
                    term = wk * ins[q + k]
                    accs[q] = term if accs[q] is None else accs[q] + term
            for q in range(CONV_TOK_BLOCK):
                cv_scr[pl.ds(base + q * SUBLANES, SUBLANES), lanes] = accs[q]
            return carry

        lax.fori_loop(0, tb // CONV_TOK_BLOCK, conv_block, 0)

    cv = cv_scr[...]
    mu = jnp.mean(cv, axis=-1, keepdims=True)
    cen = cv - mu
    var = jnp.mean(cen * cen, axis=-1, keepdims=True)
    ln = cen * lax.rsqrt(var + LN_EPS) * lng_ref[...] + lnb_ref[...]
    yc = jnp.dot(_silu(ln).astype(BF16), cpw_ref[...], preferred_element_type=F32) + cpwb_ref[...]
    cat_c = (yc * _silu(zc)).astype(BF16)
    out = out + jnp.dot(cat_c, wout_ref[2 * w:3 * w, :], preferred_element_type=F32)

    yd = (jnp.dot(p_ref[...].astype(BF16), dw_ref[...], preferred_element_type=F32)
          + db_ref[...]) * dsc_ref[...]
    cat_d = (yd * _silu(zd)).astype(BF16)
    out = out + jnp.dot(cat_d, wout_ref[3 * w:4 * w, :], preferred_element_type=F32)

    res = x_main.reshape(tb, SUBLANES, d) + gt_ref[...][None] * out.reshape(tb, SUBLANES, d)
    res = res.reshape(rows, d)
    if final_norm:
        ms = jnp.mean(res * res, axis=-1, keepdims=True)
        res = res * lax.rsqrt(ms + RMS_EPS) * fg_ref[...]
    o_ref[...] = res


def _mix_call(xt, sc, sh, gt, g, wc, wo, bconv, cconv, lng, lnb, cpw, cpwb, dw, db, dsc, wout,
              yf, yb, p, fg, tb, final_norm):
    n_rows, d = xt.shape
    rows = tb * SUBLANES
    halo = MIX_HALO_TOK * SUBLANES
    n = n_rows // rows
    per = rows // halo
    last_halo = n_rows // halo - 1
    w = BRANCH_W
    const2 = lambda j: (0, 0)
    const3 = lambda j: (0, 0, 0)
    chunk = lambda j: (j, 0)

    def full(arr):
        return pl.BlockSpec(arr.shape, const2 if arr.ndim == 2 else const3)

    in_specs = [
        pl.BlockSpec((rows, d), chunk),
        pl.BlockSpec((halo, d), lambda j: (jnp.maximum(j * per - 1, 0), 0)),
        pl.BlockSpec((halo, d), lambda j: (jnp.minimum((j + 1) * per, last_halo), 0)),
        full(sc), full(sh), full(gt), full(g), full(wc), full(wo), full(bconv), full(cconv),
        full(lng), full(lnb), full(cpw), full(cpwb), full(dw), full(db), full(dsc), full(wout),
        pl.BlockSpec((rows, w), chunk),
        pl.BlockSpec((rows, w), chunk),
        pl.BlockSpec((rows, w), chunk),
        full(fg),
    ]
    return pl.pallas_call(
        functools.partial(_mix_body, tb, final_norm),
        grid=(n,),
        in_specs=in_specs,
        out_specs=pl.BlockSpec((rows, d), chunk),
        out_shape=jax.ShapeDtypeStruct((n_rows, d), F32),
        scratch_shapes=[pltpu.VMEM((rows + 2 * halo, w), F32),
                        pltpu.VMEM((rows + 2 * halo, w), F32),
                        pltpu.VMEM((rows, w), F32)],
        compiler_params=pltpu.CompilerParams(
            dimension_semantics=("arbitrary",), vmem_limit_bytes=VMEM_LIMIT),
        name="mix",
    )(xt, xt, xt, sc, sh, gt, g, wc, wo, bconv, cconv, lng, lnb, cpw, cpwb, dw, db, dsc, wout,
      yf, yb, p, fg)


def _block_diag(blocks):
    gcount, i, jdim = blocks.shape
    eye = jnp.eye(gcount, dtype=blocks.dtype)
    return (eye[:, None, :, None] * blocks[:, :, None, :]).reshape(gcount * i, gcount * jdim)


def _time_major(a):
    b, t, d = a.shape
    return a.transpose(1, 0, 2).reshape(t * b, d)


def kernel(x, c, ctx, c_ctx, mod_w, mod_b, norm_g, w_in, w_out, a_conv, a_wr, a_br, a_wi, a_bi,
           a_lam, b_conv, c_conv, c_ln_g, c_ln_b, c_pw, c_pw_b, d_w, d_b, d_scale, final_g):
    bsz, seq, d = x.shape
    ctx_len = ctx.shape[1]
    depth = w_in.shape[0]
    w = BRANCH_W
    assert bsz == SUBLANES and d == 2 * w
    assert seq % GRID_W == 0 and seq % 64 == 0 and ctx_len % 64 == 0
    tok = 64

    xt = _time_major(x)
    ct = _time_major(ctx)
    cc = jnp.concatenate([c, jnp.broadcast_to(c_ctx[None, :], (SUBLANES, d))], axis=0)
    mod = _modulation(cc, mod_w, mod_b)

    slot = lambda l, s: w_in[l][:, s * w:(s + 1) * w]
    fg = final_g.reshape(1, d)
    zeros_state = jnp.zeros((2, SUBLANES, w), F32)

    for l in range(depth):
        last = l == depth - 1
        g = norm_g[l].reshape(1, d)
        wv = slot(l, 0).astype(BF16)
        wd = slot(l, 9).astype(BF16)
        wc = jnp.concatenate([slot(l, 3), slot(l, 4), slot(l, 6), slot(l, 7)], axis=1).astype(BF16)
        wo = jnp.concatenate([slot(l, 1), slot(l, 2), slot(l, 5), slot(l, 8), slot(l, 10)],
                             axis=1).astype(BF16)
        wg = jnp.stack([jnp.concatenate([_block_diag(a_wr[l, i]), _block_diag(a_wi[l, i])], axis=1)
                        for i in range(2)]).astype(BF16)
        bg = jnp.concatenate([a_br[l], a_bi[l]], axis=1).reshape(2, 1, 2 * w)
        lam = a_lam[l].reshape(2, 1, w)
        cconv = jnp.broadcast_to(c_conv[l][:, None, :], (C_CONV, SUBLANES, w))
        dw = _block_diag(d_w[l]).astype(BF16)
        shared = dict(
            g=g, wc=wc, wo=wo, bconv=b_conv[l], cconv=cconv, lng=c_ln_g[l].reshape(1, w),
            lnb=c_ln_b[l].reshape(1, w), cpw=c_pw[l].astype(BF16), cpwb=c_pw_b[l].reshape(1, w),
            dw=dw, db=d_b[l].reshape(1, w), dsc=d_scale[l].reshape(1, w),
            wout=w_out[l].astype(BF16), fg=fg, tb=tok)

        def split(rows):
            return rows[:, 0:d], rows[:, d:2 * d], rows[:, 2 * d:3 * d]

        sh_k, sc_k, gt_k = split(mod[l, SUBLANES:2 * SUBLANES])
        sh_x, sc_x, gt_x = split(mod[l, 0:SUBLANES])

        yf, yb, dv, state_k = _rglru_call(ct, sc_k, sh_k, g, wv, wd, a_conv[l], wg, bg, lam,
                                          zeros_state, tok)
        if not last:
            ct = _mix_call(ct, sc_k, sh_k, gt_k, yf=yf, yb=yb, p=_pool_seq_call(dv),
                           final_norm=False, **shared)
        yf, yb, dv, _ = _rglru_call(xt, sc_x, sh_x, g, wv, wd, a_conv[l], wg, bg, lam, state_k, tok)
        xt = _mix_call(xt, sc_x, sh_x, gt_x, yf=yf, yb=yb, p=_pool_grid_call(dv),
                       final_norm=last, **shared)

    return xt.reshape(seq, bsz, d).transpose(1, 0, 2)
```

```python
import functools

import jax
import jax.numpy as jnp
from jax import lax
from jax.experimental import pallas as pl
from jax.experimental.pallas import tpu as pltpu

F32 = jnp.float32
BF16 = jnp.bfloat16

SUBLANES = 8
LANES = 128
MXU_TILE = 256
BRANCH_W = 512
A_CONV = 4
B_CONV = 3
C_CONV = 31
RG_C = 8.0
GRID_W = 64
POOL_WINDOWS = (2, 4, 8, 16)
RMS_EPS = 1e-6
LN_EPS = 1e-5

A_HALO_TOK = 4
B_HALO_TOK = B_CONV // 2
C_HALO_TOK = 16
CONV_TOK_BLOCK = 8
TOK = 64
TILE_UNROLL = 4
VMEM_LIMIT = 60 * 1024 * 1024

LANE_BLOCKS = BRANCH_W // LANES
HALVES = BRANCH_W // MXU_TILE
MIX_CONV_SLOTS = (3, 4, 6, 7)
MIX_GATE_SLOTS = (1, 2, 5, 8, 10)
N_CONV_TILES = len(MIX_CONV_SLOTS) * HALVES
N_GATE_TILES = len(MIX_GATE_SLOTS) * HALVES


def _tanh_gate(v):
    return 1.0 + jnp.tanh(v)


def _silu(v):
    return (0.5 * v) * _tanh_gate(0.5 * v)


def _norm_mod(xv, g, sc, sh):
    rows, d = xv.shape
    ms = jnp.mean(xv * xv, axis=-1, keepdims=True)
    y = xv * lax.rsqrt(ms + RMS_EPS) * g
    y3 = y.reshape(rows // SUBLANES, SUBLANES, d)
    hmod = y3 * (1.0 + sc)[None] + sh[None]
    return hmod.reshape(rows, d)


def _layer_spec(arr, l):
    zeros = (0,) * (arr.ndim - 1)
    return pl.BlockSpec((None,) + arr.shape[1:], lambda j: (l,) + zeros)


def _mod_spec(mod, l, stream, part):
    d = mod.shape[3] // 3
    return pl.BlockSpec((None, None, SUBLANES, d), lambda j: (l, stream, 0, part))


def _mod_body(c_ref, w_ref, b_ref, o_ref):
    s = _silu(c_ref[...])
    o_ref[0] = jnp.dot(s, w_ref[0], preferred_element_type=F32,
                       precision=lax.Precision.HIGHEST) + b_ref[0]


def _modulation(cc, mod_w, mod_b):
    depth, d, d3 = mod_w.shape
    nj = d3 // d
    rows = cc.shape[0]
    return pl.pallas_call(
        _mod_body,
        grid=(depth, nj),
        in_specs=[pl.BlockSpec((rows, d), lambda l, j: (0, 0)),
                  pl.BlockSpec((1, d, d), lambda l, j: (l, 0, j)),
                  pl.BlockSpec((1, 1, d), lambda l, j: (l, 0, j))],
        out_specs=pl.BlockSpec((1, rows, d), lambda l, j: (l, 0, j)),
        out_shape=jax.ShapeDtypeStruct((depth, rows, d3), F32),
        compiler_params=pltpu.CompilerParams(
            dimension_semantics=("arbitrary", "arbitrary"), vmem_limit_bytes=VMEM_LIMIT),
        name="modulation",
    )(cc, mod_w, mod_b.reshape(depth, 1, d3))


def _rglru_coeffs(xc, wg_ref, bg_ref, lam_ref, n, a_scr, b_scr):
    nl = -lam_ref[n]
    softplus = jnp.maximum(nl, 0.0) + jnp.log1p(jnp.exp(-jnp.abs(nl)))
    half_c = (-0.5 * RG_C) * softplus
    xb = xc.astype(BF16)
    bg = bg_ref[n]
    w = xc.shape[1]
    for hb in range(HALVES):
        cols = slice(hb * MXU_TILE, (hb + 1) * MXU_TILE)
        icols = slice(w + hb * MXU_TILE, w + (hb + 1) * MXU_TILE)
        xh = xb[:, cols]
        gr = jnp.dot(xh, wg_ref[n, hb], preferred_element_type=F32) + bg[:, cols]
        gi = jnp.dot(xh, wg_ref[n, HALVES + hb], preferred_element_type=F32) + bg[:, icols]
        log_a = half_c[:, cols] * _tanh_gate(gr)
        a = jnp.exp(log_a)
        s = -jnp.tanh(log_a) * (1.0 + a * a)
        root = jnp.where(s > 0.0, s * lax.rsqrt(s), 0.0)
        a_scr[n, :, cols] = a
        b_scr[n, :, cols] = root * (_tanh_gate(gi) * (0.5 * xc[:, cols]))


def _rglru_body(ta, xf_ref, xfp_ref, xb_ref, xbn_ref, sh_ref, sc_ref, g_ref, wa_ref,
                aconv_ref, wg_ref, bg_ref, lam_ref, h0_ref,
                yf_ref, yb_ref, dv_ref, hfin_ref, st_ref, a_scr, b_scr):
    j = pl.program_id(0)
    rows = ta * SUBLANES
    halo = A_HALO_TOK * SUBLANES
    w = BRANCH_W

    @pl.when(j == 0)
    def _():
        st_ref[...] = h0_ref[...]

    g = g_ref[...]
    sc = sc_ref[...]
    sh = sh_ref[...]
    inner = j > 0

    hf = _norm_mod(jnp.concatenate([xfp_ref[...], xf_ref[...]], axis=0), g, sc, sh).astype(BF16)
    va = jnp.dot(hf, wa_ref[:, 0:w], preferred_element_type=F32)
    dv_ref[...] = jnp.dot(hf[halo:], wa_ref[:, w:2 * w], preferred_element_type=F32)
    va = jnp.concatenate([jnp.where(inner, va[:halo], 0.0), va[halo:]], axis=0)
    xc = None
    for k in range(A_CONV):
        off = (A_HALO_TOK - (A_CONV - 1) + k) * SUBLANES
        term = aconv_ref[0, k:k + 1, :] * va[off:off + rows]
        xc = term if xc is None else xc + term
    _rglru_coeffs(xc, wg_ref, bg_ref, lam_ref, 0, a_scr, b_scr)

    hb = _norm_mod(jnp.concatenate([xb_ref[...], xbn_ref[...]], axis=0), g, sc, sh).astype(BF16)
    vb = jnp.dot(hb, wa_ref[:, 0:w], preferred_element_type=F32)
    vb = jnp.concatenate([vb[:rows], jnp.where(inner, vb[rows:], 0.0)], axis=0)
    xc = None
    for k in range(A_CONV):
        off = k * SUBLANES
        term = aconv_ref[1, k:k + 1, :] * vb[off:off + rows]
        xc = term if xc is None else xc + term
    _rglru_coeffs(xc, wg_ref, bg_ref, lam_ref, 1, a_scr, b_scr)

    def step(i, carry):
        h_f, h_b = carry
        rf = pl.multiple_of(i * SUBLANES, SUBLANES)
        rb = pl.multiple_of((ta - 1 - i) * SUBLANES, SUBLANES)
        h_f = a_scr[0, pl.ds(rf, SUBLANES), :] * h_f + b_scr[0, pl.ds(rf, SUBLANES), :]
        yf_ref[pl.ds(rf, SUBLANES), :] = h_f
        h_b = a_scr[1, pl.ds(rb, SUBLANES), :] * h_b + b_scr[1, pl.ds(rb, SUBLANES), :]
        yb_ref[pl.ds(rb, SUBLANES), :] = h_b
        return h_f, h_b

    h_f, h_b = lax.fori_loop(0, ta, step, (st_ref[0], st_ref[1]), unroll=8)
    st_ref[0] = h_f
    st_ref[1] = h_b
    hfin_ref[0] = h_f
    hfin_ref[1] = h_b


def _rglru_call(xt, mod, stream, l, p, h0):
    n_rows, d = xt.shape
    rows = TOK * SUBLANES
    halo = A_HALO_TOK * SUBLANES
    n = n_rows // rows
    per = rows // halo
    last_halo = n_rows // halo - 1
    w = BRANCH_W
    in_specs = [
        pl.BlockSpec((rows, d), lambda j: (j, 0)),
        pl.BlockSpec((halo, d), lambda j: (jnp.maximum(j * per - 1, 0), 0)),
        pl.BlockSpec((rows, d), lambda j: (n - 1 - j, 0)),
        pl.BlockSpec((halo, d), lambda j: (jnp.minimum((n - j) * per, last_halo), 0)),
        _mod_spec(mod, l, stream, 0), _mod_spec(mod, l, stream, 1),
        _layer_spec(p["g"], l), _layer_spec(p["wa"], l), _layer_spec(p["aconv"], l),
        _layer_spec(p["wg"], l), _layer_spec(p["bg"], l), _layer_spec(p["lam"], l),
        pl.BlockSpec((2, SUBLANES, w), lambda j: (0, 0, 0)),
    ]
    out_specs = [
        pl.BlockSpec((rows, w), lambda j: (j, 0)),
        pl.BlockSpec((rows, w), lambda j: (n - 1 - j, 0)),
        pl.BlockSpec((rows, w), lambda j: (j, 0)),
        pl.BlockSpec((2, SUBLANES, w), lambda j: (0, 0, 0)),
    ]
    out_shape = [jax.ShapeDtypeStruct((n_rows, w), F32)] * 3 + [
        jax.ShapeDtypeStruct((2, SUBLANES, w), F32)]
    return pl.pallas_call(
        functools.partial(_rglru_body, TOK),
        grid=(n,),
        in_specs=in_specs,
        out_specs=out_specs,
        out_shape=out_shape,
        scratch_shapes=[pltpu.VMEM((2, SUBLANES, w), F32),
                        pltpu.VMEM((2, rows, w), F32),
                        pltpu.VMEM((2, rows, w), F32)],
        compiler_params=pltpu.CompilerParams(
            dimension_semantics=("arbitrary",), vmem_limit_bytes=VMEM_LIMIT),
        name="rglru",
    )(xt, xt, xt, xt, mod, mod, p["g"], p["wa"], p["aconv"], p["wg"], p["bg"], p["lam"], h0)


def _window_sums(padded, w):
    f = padded
    k = 1
    while k < w:
        f = f[:-k] + f[k:]
        k *= 2
    return f


def _window_count(pos, half, size):
    return jnp.minimum(pos + half, size) - jnp.maximum(pos - half, 0)


def _pool_grid_one(v_ref, o_ref, p_scr, w):
    grid_rows = v_ref.shape[0]
    half = w // 2
    tile = v_ref.shape[2:]
    zpad = jnp.zeros((half,) + tile, F32)
    p_scr[0:half] = jnp.zeros((half, GRID_W) + tile, F32)
    p_scr[half + grid_rows:2 * half + grid_rows] = jnp.zeros((half, GRID_W) + tile, F32)

    def row_body(r, carry):
        f = _window_sums(jnp.concatenate([zpad, v_ref[r], zpad], axis=0), w)
        p_scr[half + r] = f[:GRID_W]
        return carry

    lax.fori_loop(0, grid_rows, row_body, 0)

    rpos = lax.broadcasted_iota(jnp.int32, (grid_rows,) + tile, 0)
    rcnt = _window_count(rpos, half, grid_rows)

    def col_body(c, carry):
        f = _window_sums(p_scr[pl.ds(0, grid_rows + 2 * half), c], w)
        cnt = (rcnt * _window_count(c, half, GRID_W)).astype(F32)
        o_ref[:, c] = f[:grid_rows] / cnt - v_ref[:, c]
        return carry

    lax.fori_loop(0, GRID_W, col_body, 0)


def _pool_grid_body(v_ref, o_ref, p_scr):
    gidx = pl.program_id(0)
    for gi, w in enumerate(POOL_WINDOWS):
        @pl.when(gidx == gi)
        def _(w=w):
            _pool_grid_one(v_ref, o_ref, p_scr, w)


def _pool_grid_call(dv):
    n_rows, w = dv.shape
    grid_rows = n_rows // (SUBLANES * GRID_W)
    v4 = dv.reshape(grid_rows, GRID_W, SUBLANES, w)
    blk = (grid_rows, GRID_W, SUBLANES, LANES)
    spec = pl.BlockSpec(blk, lambda gidx: (0, 0, 0, gidx))
    out = pl.pallas_call(
        _pool_grid_body,
        grid=(len(POOL_WINDOWS),),
        in_specs=[spec],
        out_specs=spec,
        out_shape=jax.ShapeDtypeStruct(v4.shape, F32),
        scratch_shapes=[pltpu.VMEM((grid_rows + max(POOL_WINDOWS), GRID_W, SUBLANES, LANES), F32)],
        compiler_params=pltpu.CompilerParams(
            dimension_semantics=("arbitrary",), vmem_limit_bytes=VMEM_LIMIT),
        name="pool_grid",
    )(v4)
    return out.reshape(n_rows, w)


def _pool_seq_body(v_ref, o_ref):
    gidx = pl.program_id(0)
    t = v_ref.shape[0]
    tile = v_ref.shape[1:]
    for gi, w in enumerate(POOL_WINDOWS):
        @pl.when(gidx == gi)
        def _(w=w):
            half = w // 2
            v = v_ref[...]
            zpad = jnp.zeros((half,) + tile, F32)
            f = _window_sums(jnp.concatenate([zpad, v, zpad], axis=0), w)
            pos = lax.broadcasted_iota(jnp.int32, (t,) + tile, 0)
            cnt = _window_count(pos, half, t).astype(F32)
            o_ref[...] = f[:t] / cnt - v


def _pool_seq_call(dv):
    n_rows, w = dv.shape
    t = n_rows // SUBLANES
    v3 = dv.reshape(t, SUBLANES, w)
    spec = pl.BlockSpec((t, SUBLANES, LANES), lambda gidx: (0, 0, gidx))
    out = pl.pallas_call(
        _pool_seq_body,
        grid=(len(POOL_WINDOWS),),
        in_specs=[spec],
        out_specs=spec,
        out_shape=jax.ShapeDtypeStruct(v3.shape, F32),
        compiler_params=pltpu.CompilerParams(
            dimension_semantics=("arbitrary",), vmem_limit_bytes=VMEM_LIMIT),
        name="pool_seq",
    )(v3)
    return out.reshape(n_rows, w)


CONV_FIRST = C_HALO_TOK - C_CONV // 2


def _conv31_block(cb_ref, cconv_ref, cv_ref, lb, row0):
    accs = [None] * CONV_TOK_BLOCK
    taps = {}
    for m in range(CONV_TOK_BLOCK + C_CONV - 1):
        tile = cb_ref[lb, pl.ds(row0 + (CONV_FIRST + m) * SUBLANES, SUBLANES), :]
        for q in range(CONV_TOK_BLOCK):
            k = m - q
            if 0 <= k < C_CONV:
                if k not in taps:
                    taps[k] = cconv_ref[lb, k]
                term = taps[k] * tile
                accs[q] = term if accs[q] is None else accs[q] + term
        taps.pop(m - CONV_TOK_BLOCK + 1, None)
    for q in range(CONV_TOK_BLOCK):
        cv_ref[lb, pl.ds(row0 + q * SUBLANES, SUBLANES), :] = accs[q]


def _conv31_head_fixup(head, cconv_ref, cv_ref, tb):
    reach = C_CONV // 2
    for lb in range(LANE_BLOCKS):
        lanes = slice(lb * LANES, (lb + 1) * LANES)
        for i in range(reach):
            pos = tb - reach + i
            add = None
            for k in range(C_CONV - 1 - i, C_CONV):
                t = k - (C_CONV - 1 - i)
                term = cconv_ref[lb, k] * head[t * SUBLANES:(t + 1) * SUBLANES, lanes]
                add = term if add is None else add + term
            r0 = pos * SUBLANES
            cv_ref[lb, r0:r0 + SUBLANES, :] = cv_ref[lb, r0:r0 + SUBLANES, :] + add


def _mix_body(tb, n, final_norm, xp_ref, xr_ref, sh_ref, sc_ref, gt_ref, g_ref, wt_ref,
              bconv_ref, cconv_ref, lng_ref, lnb_ref, cpw_ref, cpwb_ref, dw_ref, db_ref, dsc_ref,
              wout_ref, yf_ref, yb_ref, p_ref, fg_ref, o_ref,
              hnew_scr, projc_scr, projz_scr, curb_scr, tailb_scr, cbb_scr,
              curc_scr, tailc_scr, cbc_scr, cv_scr):
    j = pl.program_id(0)
    rows = tb * SUBLANES
    hb = B_HALO_TOK * SUBLANES
    hc = C_HALO_TOK * SUBLANES
    d = xp_ref.shape[1]
    gen = j % 2
    blocks_per_lane = tb // CONV_TOK_BLOCK
    blocks_per_tile = LANE_BLOCKS * blocks_per_lane // (N_CONV_TILES + N_GATE_TILES - HALVES)

    @pl.when(j == 0)
    def _():
        projz_scr[1] = jnp.zeros(projz_scr.shape[1:], F32)
        curb_scr[...] = jnp.zeros(curb_scr.shape, F32)
        tailb_scr[...] = jnp.zeros(tailb_scr.shape, F32)
        curc_scr[...] = jnp.zeros(curc_scr.shape, F32)
        tailc_scr[...] = jnp.zeros(tailc_scr.shape, F32)

    hnew_scr[...] = _norm_mod(xp_ref[...], g_ref[...], sc_ref[...], sh_ref[...]).astype(BF16)

    old = 1 - gen

    def slot(s, h):
        return projz_scr[old, s * HALVES + h]

    def wout_rows(branch, h):
        r0 = branch * BRANCH_W + h * MXU_TILE
        return wout_ref[r0:r0 + MXU_TILE, :]

    def half(v, h):
        return v[:, h * MXU_TILE:(h + 1) * MXU_TILE]

    cbc_scr[:, 0:hc, :] = tailc_scr[...]
    cbc_scr[:, hc:hc + rows, :] = curc_scr[...]
    cbc_scr[:, hc + rows:, :] = jnp.zeros((LANE_BLOCKS, hc, LANES), F32)

    def conv_blocks(first_block, i):
        for r in range(blocks_per_tile):
            blk = first_block + i * blocks_per_tile + r
            lb = blk // blocks_per_lane
            row0 = pl.multiple_of((blk % blocks_per_lane) * (CONV_TOK_BLOCK * SUBLANES),
                                  CONV_TOK_BLOCK * SUBLANES)
            _conv31_block(cbc_scr, cconv_ref, cv_scr, lb, row0)

    def conv_tile(i, carry):
        projc_scr[i] = jnp.dot(hnew_scr[...], wt_ref[i], preferred_element_type=F32)
        conv_blocks(0, i)
        return carry

    lax.fori_loop(0, N_CONV_TILES, conv_tile, 0, unroll=TILE_UNROLL)

    def gate_tile(i, carry):
        projz_scr[gen, i] = jnp.dot(hnew_scr[...], wt_ref[N_CONV_TILES + i],
                                    preferred_element_type=F32)
        conv_blocks(N_CONV_TILES * blocks_per_tile, i)
        return carry

    lax.fori_loop(0, N_GATE_TILES - HALVES, gate_tile, 0, unroll=TILE_UNROLL)
    for i in range(N_GATE_TILES - HALVES, N_GATE_TILES):
        projz_scr[gen, i] = jnp.dot(hnew_scr[...], wt_ref[N_CONV_TILES + i],
                                    preferred_element_type=F32)

    has_next = j < n
    ub_new = jnp.concatenate([projc_scr[h] * projc_scr[HALVES + h] for h in range(HALVES)], axis=1)
    uc_new = jnp.concatenate(
        [(0.5 * projc_scr[2 * HALVES + h]) * _tanh_gate(projc_scr[3 * HALVES + h])
         for h in range(HALVES)], axis=1)
    _conv31_head_fixup(jnp.where(has_next, uc_new[0:hc], 0.0), cconv_ref, cv_scr, tb)
    cbb_scr[0:hb] = tailb_scr[...]
    cbb_scr[hb:hb + rows] = curb_scr[...]
    cbb_scr[hb + rows:] = jnp.where(has_next, ub_new[0:hb], 0.0)
    tailb_scr[...] = curb_scr[rows - hb:rows]
    curb_scr[...] = ub_new
    tailc_scr[...] = curc_scr[:, rows - hc:rows, :]
    for lb in range(LANE_BLOCKS):
        curc_scr[lb] = uc_new[:, lb * LANES:(lb + 1) * LANES]

    conv_b = None
    for k in range(B_CONV):
        term = bconv_ref[k:k + 1, :] * cbb_scr[k * SUBLANES:k * SUBLANES + rows, :]
        conv_b = term if conv_b is None else conv_b + term

    cv = jnp.concatenate([cv_scr[lb] for lb in range(LANE_BLOCKS)], axis=1)
    mu = jnp.mean(cv, axis=-1, keepdims=True)
    cen = cv - mu
    var = jnp.mean(cen * cen, axis=-1, keepdims=True)
    ln = cen * lax.rsqrt(var + LN_EPS) * lng_ref[...] + lnb_ref[...]
    yc = jnp.dot(_silu(ln).astype(BF16), cpw_ref[...], preferred_element_type=F32) + cpwb_ref[...]

    ya = yf_ref[...] + yb_ref[...]
    pooled = p_ref[...].astype(BF16)
    out = None
    for h in range(HALVES):
        za, bb, zb, zc, zd = (slot(s, h) for s in range(len(MIX_GATE_SLOTS)))
        yd = (jnp.dot(half(pooled, h), dw_ref[h], preferred_element_type=F32)
              + half(db_ref[...], h)) * half(dsc_ref[...], h)
        cats = (half(ya, h) * (za * _tanh_gate(za)),
                bb * half(conv_b, h) * (zb * _tanh_gate(zb)),
                half(yc, h) * (zc * _tanh_gate(zc)),
                yd * (zd * _tanh_gate(zd)))
        for branch, cat in enumerate(cats):
            term = jnp.dot(cat.astype(BF16), wout_rows(branch, h), preferred_element_type=F32)
            out = term if out is None else out + term

    res = xr_ref[...].reshape(tb, SUBLANES, d) + gt_ref[...][None] * out.reshape(tb, SUBLANES, d)
    res = res.reshape(rows, d)
    if final_norm:
        ms = jnp.mean(res * res, axis=-1, keepdims=True)
        res = res * lax.rsqrt(ms + RMS_EPS) * fg_ref[...]
    o_ref[...] = res


def _mix_call(xt, mod, stream, l, p, yf, yb, pooled, fg, final_norm):
    n_rows, d = xt.shape
    rows = TOK * SUBLANES
    n = n_rows // rows
    w = BRANCH_W
    hb = B_HALO_TOK * SUBLANES
    hc = C_HALO_TOK * SUBLANES
    assert (LANE_BLOCKS * (TOK // CONV_TOK_BLOCK)) % (N_CONV_TILES + N_GATE_TILES - HALVES) == 0
    ahead = lambda j: (jnp.minimum(j, n - 1), 0)
    behind = lambda j: (jnp.maximum(j - 1, 0), 0)
    names = ["g", "wt", "bconv", "cconv", "lng", "lnb", "cpw", "cpwb", "dw", "db", "dsc", "wout"]
    in_specs = (
        [pl.BlockSpec((rows, d), ahead), pl.BlockSpec((rows, d), behind),
         _mod_spec(mod, l, stream, 0), _mod_spec(mod, l, stream, 1), _mod_spec(mod, l, stream, 2)]
        + [_layer_spec(p[k], l) for k in names]
        + [pl.BlockSpec((rows, w), behind)] * 3
        + [pl.BlockSpec(fg.shape, lambda j: (0, 0))])
    return pl.pallas_call(
        functools.partial(_mix_body, TOK, n, final_norm),
        grid=(n + 1,),
        in_specs=in_specs,
        out_specs=pl.BlockSpec((rows, d), behind),
        out_shape=jax.ShapeDtypeStruct((n_rows, d), F32),
        scratch_shapes=[pltpu.VMEM((rows, d), BF16),
                        pltpu.VMEM((N_CONV_TILES, rows, MXU_TILE), F32),
                        pltpu.VMEM((2, N_GATE_TILES, rows, MXU_TILE), F32),
                        pltpu.VMEM((rows, w), F32), pltpu.VMEM((hb, w), F32),
                        pltpu.VMEM((rows + 2 * hb, w), F32),
                        pltpu.VMEM((LANE_BLOCKS, rows, LANES), F32),
                        pltpu.VMEM((LANE_BLOCKS, hc, LANES), F32),
                        pltpu.VMEM((LANE_BLOCKS, rows + 2 * hc, LANES), F32),
                        pltpu.VMEM((LANE_BLOCKS, rows, LANES), F32)],
        compiler_params=pltpu.CompilerParams(
            dimension_semantics=("arbitrary",), vmem_limit_bytes=VMEM_LIMIT),
        name="mix",
    )(xt, xt, mod, mod, mod, *[p[k] for k in names], yf, yb, pooled, fg)


def _block_diag(blocks):
    gcount, i, jdim = blocks.shape[-3:]
    eye = jnp.eye(gcount, dtype=blocks.dtype)
    dense = eye[:, None, :, None] * blocks[..., :, :, None, :]
    return dense.reshape(blocks.shape[:-3] + (gcount * i, gcount * jdim))


def _diag_tiles(blocks):
    gcount, i = blocks.shape[-3:-1]
    per_tile = MXU_TILE // i
    grouped = blocks.reshape(blocks.shape[:-3] + (gcount // per_tile, per_tile, i, i))
    return _block_diag(grouped)


def _time_major(a):
    b, t, d = a.shape
    return a.transpose(1, 0, 2).reshape(t * b, d)


def _prepare_params(norm_g, w_in, w_out, a_conv, a_wr, a_br, a_wi, a_bi, a_lam, b_conv, c_conv,
                    c_ln_g, c_ln_b, c_pw, c_pw_b, d_w, d_b, d_scale):
    depth, d, _ = w_in.shape
    w = BRANCH_W
    w4 = w_in.reshape(depth, d, -1, w)
    scale = {1: 0.5, 5: 0.5, 7: 0.5, 8: 0.5, 10: 0.5}

    def take(slots):
        cols = [w4[:, :, s, :] * scale.get(s, 1.0) for s in slots]
        return jnp.concatenate(cols, axis=-1).astype(BF16)

    wt = take(MIX_CONV_SLOTS + MIX_GATE_SLOTS)
    wt = wt.reshape(depth, d, -1, MXU_TILE).transpose(0, 2, 1, 3)
    gates = jnp.stack([a_wr, a_wi], axis=2)
    wg = (0.5 * _diag_tiles(gates)).reshape(depth, 2, 2 * HALVES, MXU_TILE, MXU_TILE).astype(BF16)
    cconv = c_conv.reshape(depth, C_CONV, LANE_BLOCKS, 1, LANES).transpose(0, 2, 1, 3, 4)
    return dict(
        g=norm_g.reshape(depth, 1, d),
        wa=take((0, 9)), wt=wt,
        aconv=a_conv, wg=wg,
        bg=(0.5 * jnp.concatenate([a_br, a_bi], axis=-1)).reshape(depth, 2, 1, 2 * w),
        lam=a_lam.reshape(depth, 2, 1, w),
        bconv=b_conv,
        cconv=jnp.broadcast_to(cconv, (depth, LANE_BLOCKS, C_CONV, SUBLANES, LANES)),
        lng=c_ln_g.reshape(depth, 1, w), lnb=c_ln_b.reshape(depth, 1, w),
        cpw=c_pw.astype(BF16), cpwb=c_pw_b.reshape(depth, 1, w),
        dw=_diag_tiles(d_w).astype(BF16), db=d_b.reshape(depth, 1, w),
        dsc=d_scale.reshape(depth, 1, w),
        wout=w_out.astype(BF16))


def kernel(x, c, ctx, c_ctx, mod_w, mod_b, norm_g, w_in, w_out, a_conv, a_wr, a_br, a_wi, a_bi,
           a_lam, b_conv, c_conv, c_ln_g, c_ln_b, c_pw, c_pw_b, d_w, d_b, d_scale, final_g):
    bsz, seq, d = x.shape
    ctx_len = ctx.shape[1]
    depth = w_in.shape[0]
    w = BRANCH_W
    assert bsz == SUBLANES and d == 2 * w
    assert seq % GRID_W == 0 and seq % TOK == 0 and ctx_len % TOK == 0

    xt = _time_major(x)
    ct = _time_major(ctx)
    cc = jnp.concatenate([c, jnp.broadcast_to(c_ctx[None, :], (SUBLANES, d))], axis=0)
    mod = _modulation(cc, mod_w, mod_b).reshape(depth, 2, SUBLANES, 3 * d)
    p = _prepare_params(norm_g, w_in, w_out, a_conv, a_wr, a_br, a_wi, a_bi, a_lam, b_conv, c_conv,
                        c_ln_g, c_ln_b, c_pw, c_pw_b, d_w, d_b, d_scale)
    fg = final_g.reshape(1, d)
    zeros_state = jnp.zeros((2, SUBLANES, w), F32)

    for l in range(depth):
        last = l == depth - 1
        yf, yb, dv, state_k = _rglru_call(ct, mod, 1, l, p, zeros_state)
        if not last:
            ct = _mix_call(ct, mod, 1, l, p, yf, yb, _pool_seq_call(dv), fg, False)
        yf, yb, dv, _ = _rglru_call(xt, mod, 0, l, p, state_k)
        xt = _mix_call(xt, mod, 0, l, p, yf, yb, _pool_grid_call(dv), fg, last)

    return xt.reshape(seq, bsz, d).transpose(1, 0, 2)
```

```python
import functools

import jax
import jax.numpy as jnp
from jax import lax
from jax.experimental import pallas as pl
from jax.experimental.pallas import tpu as pltpu

F32 = jnp.float32
BF16 = jnp.bfloat16

SUBLANES = 8
LANES = 128
MXU_TILE = 256
BRANCH_W = 512
A_CONV = 4
B_CONV = 3
C_CONV = 31
RG_C = 8.0
GRID_W = 64
POOL_WINDOWS = (2, 4, 8, 16)
RMS_EPS = 1e-6
LN_EPS = 1e-5

A_HALO_TOK = 4
B_HALO_TOK = B_CONV // 2
C_HALO_TOK = 16
CONV_TOK_BLOCK = 8
TOK = 64
TILE_UNROLL = 4
VMEM_LIMIT = 60 * 1024 * 1024

LANE_BLOCKS = BRANCH_W // LANES
HALVES = BRANCH_W // MXU_TILE
MIX_CONV_SLOTS = (3, 4, 6, 7)
MIX_GATE_SLOTS = (1, 2, 5, 8, 10)
N_CONV_TILES = len(MIX_CONV_SLOTS) * HALVES
N_GATE_TILES = len(MIX_GATE_SLOTS) * HALVES
TILE_SLOTS = MIX_CONV_SLOTS + MIX_GATE_SLOTS + (0, 9)
WV_TILE0 = N_CONV_TILES + N_GATE_TILES
WD_TILE0 = WV_TILE0 + HALVES
HALVED_SLOTS = (1, 5, 7, 8, 10)


def _tanh_gate(v):
    return 1.0 + jnp.tanh(v)


def _silu(v):
    return (0.5 * v) * _tanh_gate(0.5 * v)


def _norm_mod(xv, g, sc, sh):
    rows, d = xv.shape
    ms = jnp.mean(xv * xv, axis=-1, keepdims=True)
    y = xv * lax.rsqrt(ms + RMS_EPS) * g
    y3 = y.reshape(rows // SUBLANES, SUBLANES, d)
    hmod = y3 * (1.0 + sc)[None] + sh[None]
    return hmod.reshape(rows, d)


def _layer_spec(arr, l):
    zeros = (0,) * (arr.ndim - 1)
    return pl.BlockSpec((None,) + arr.shape[1:], lambda j: (l,) + zeros)


def _tile_spec(wt, l, first, count):
    assert first % count == 0
    return pl.BlockSpec((None, count) + wt.shape[2:], lambda j: (l, first // count, 0, 0))


def _mod_spec(mod, l, stream, part):
    d = mod.shape[3] // 3
    return pl.BlockSpec((None, None, SUBLANES, d), lambda j: (l, stream, 0, part))


def _mod_body(c_ref, w_ref, b_ref, o_ref):
    s = _silu(c_ref[...])
    o_ref[0] = jnp.dot(s, w_ref[0], preferred_element_type=F32,
                       precision=lax.Precision.HIGHEST) + b_ref[0]


def _modulation(cc, mod_w, mod_b):
    depth, d, d3 = mod_w.shape
    nj = d3 // d
    rows = cc.shape[0]
    return pl.pallas_call(
        _mod_body,
        grid=(depth, nj),
        in_specs=[pl.BlockSpec((rows, d), lambda l, j: (0, 0)),
                  pl.BlockSpec((1, d, d), lambda l, j: (l, 0, j)),
                  pl.BlockSpec((1, 1, d), lambda l, j: (l, 0, j))],
        out_specs=pl.BlockSpec((1, rows, d), lambda l, j: (l, 0, j)),
        out_shape=jax.ShapeDtypeStruct((depth, rows, d3), F32),
        compiler_params=pltpu.CompilerParams(
            dimension_semantics=("arbitrary", "arbitrary"), vmem_limit_bytes=VMEM_LIMIT),
        name="modulation",
    )(cc, mod_w, mod_b.reshape(depth, 1, d3))


def _rglru_coeffs(xc, wg_ref, bg_ref, lam_ref, n, a_scr, b_scr):
    nl = -lam_ref[n]
    softplus = jnp.maximum(nl, 0.0) + jnp.log1p(jnp.exp(-jnp.abs(nl)))
    half_c = (-0.5 * RG_C) * softplus
    xb = xc.astype(BF16)
    bg = bg_ref[n]
    w = xc.shape[1]
    for hb in range(HALVES):
        cols = slice(hb * MXU_TILE, (hb + 1) * MXU_TILE)
        icols = slice(w + hb * MXU_TILE, w + (hb + 1) * MXU_TILE)
        xh = xb[:, cols]
        gr = jnp.dot(xh, wg_ref[n, hb], preferred_element_type=F32) + bg[:, cols]
        gi = jnp.dot(xh, wg_ref[n, HALVES + hb], preferred_element_type=F32) + bg[:, icols]
        log_a = half_c[:, cols] * _tanh_gate(gr)
        a = jnp.exp(log_a)
        s = -jnp.tanh(log_a) * (1.0 + a * a)
        root = jnp.where(s > 0.0, s * lax.rsqrt(s), 0.0)
        a_scr[n, :, cols] = a
        b_scr[n, :, cols] = root * (_tanh_gate(gi) * (0.5 * xc[:, cols]))


def _tile_dot(lhs, w_ref):
    return jnp.concatenate([jnp.dot(lhs, w_ref[t], preferred_element_type=F32)
                            for t in range(w_ref.shape[0])], axis=1)


def _rglru_body(ta, xf_ref, xfp_ref, xb_ref, xbn_ref, sh_ref, sc_ref, g_ref, wv_ref, wd_ref,
                aconv_ref, wg_ref, bg_ref, lam_ref, h0_ref,
                yf_ref, yb_ref, dv_ref, hfin_ref, st_ref, a_scr, b_scr):
    j = pl.program_id(0)
    rows = ta * SUBLANES
    halo = A_HALO_TOK * SUBLANES
    w = BRANCH_W

    @pl.when(j == 0)
    def _():
        st_ref[...] = h0_ref[...]

    g = g_ref[...]
    sc = sc_ref[...]
    sh = sh_ref[...]
    inner = j > 0

    hf = _norm_mod(jnp.concatenate([xfp_ref[...], xf_ref[...]], axis=0), g, sc, sh).astype(BF16)
    va = _tile_dot(hf, wv_ref)
    dv_ref[...] = _tile_dot(hf[halo:], wd_ref)
    va = jnp.concatenate([jnp.where(inner, va[:halo], 0.0), va[halo:]], axis=0)
    xc = None
    for k in range(A_CONV):
        off = (A_HALO_TOK - (A_CONV - 1) + k) * SUBLANES
        term = aconv_ref[0, k:k + 1, :] * va[off:off + rows]
        xc = term if xc is None else xc + term
    _rglru_coeffs(xc, wg_ref, bg_ref, lam_ref, 0, a_scr, b_scr)

    hb = _norm_mod(jnp.concatenate([xb_ref[...], xbn_ref[...]], axis=0), g, sc, sh).astype(BF16)
    vb = _tile_dot(hb, wv_ref)
    vb = jnp.concatenate([vb[:rows], jnp.where(inner, vb[rows:], 0.0)], axis=0)
    xc = None
    for k in range(A_CONV):
        off = k * SUBLANES
        term = aconv_ref[1, k:k + 1, :] * vb[off:off + rows]
        xc = term if xc is None else xc + term
    _rglru_coeffs(xc, wg_ref, bg_ref, lam_ref, 1, a_scr, b_scr)

    def step(i, carry):
        h_f, h_b = carry
        rf = pl.multiple_of(i * SUBLANES, SUBLANES)
        rb = pl.multiple_of((ta - 1 - i) * SUBLANES, SUBLANES)
        h_f = a_scr[0, pl.ds(rf, SUBLANES), :] * h_f + b_scr[0, pl.ds(rf, SUBLANES), :]
        yf_ref[pl.ds(rf, SUBLANES), :] = h_f
        h_b = a_scr[1, pl.ds(rb, SUBLANES), :] * h_b + b_scr[1, pl.ds(rb, SUBLANES), :]
        yb_ref[pl.ds(rb, SUBLANES), :] = h_b
        return h_f, h_b

    h_f, h_b = lax.fori_loop(0, ta, step, (st_ref[0], st_ref[1]), unroll=8)
    st_ref[0] = h_f
    st_ref[1] = h_b
    hfin_ref[0] = h_f
    hfin_ref[1] = h_b


def _rglru_call(xt, mod, stream, l, p, h0):
    n_rows, d = xt.shape
    rows = TOK * SUBLANES
    halo = A_HALO_TOK * SUBLANES
    n = n_rows // rows
    per = rows // halo
    last_halo = n_rows // halo - 1
    w = BRANCH_W
    in_specs = [
        pl.BlockSpec((rows, d), lambda j: (j, 0)),
        pl.BlockSpec((halo, d), lambda j: (jnp.maximum(j * per - 1, 0), 0)),
        pl.BlockSpec((rows, d), lambda j: (n - 1 - j, 0)),
        pl.BlockSpec((halo, d), lambda j: (jnp.minimum((n - j) * per, last_halo), 0)),
        _mod_spec(mod, l, stream, 0), _mod_spec(mod, l, stream, 1),
        _layer_spec(p["g"], l), _tile_spec(p["wt"], l, WV_TILE0, HALVES),
        _tile_spec(p["wt"], l, WD_TILE0, HALVES), _layer_spec(p["aconv"], l),
        _layer_spec(p["wg"], l), _layer_spec(p["bg"], l), _layer_spec(p["lam"], l),
        pl.BlockSpec((2, SUBLANES, w), lambda j: (0, 0, 0)),
    ]
    out_specs = [
        pl.BlockSpec((rows, w), lambda j: (j, 0)),
        pl.BlockSpec((rows, w), lambda j: (n - 1 - j, 0)),
        pl.BlockSpec((rows, w), lambda j: (j, 0)),
        pl.BlockSpec((2, SUBLANES, w), lambda j: (0, 0, 0)),
    ]
    out_shape = [jax.ShapeDtypeStruct((n_rows, w), F32)] * 3 + [
        jax.ShapeDtypeStruct((2, SUBLANES, w), F32)]
    return pl.pallas_call(
        functools.partial(_rglru_body, TOK),
        grid=(n,),
        in_specs=in_specs,
        out_specs=out_specs,
        out_shape=out_shape,
        scratch_shapes=[pltpu.VMEM((2, SUBLANES, w), F32),
                        pltpu.VMEM((2, rows, w), F32),
                        pltpu.VMEM((2, rows, w), F32)],
        compiler_params=pltpu.CompilerParams(
            dimension_semantics=("arbitrary",), vmem_limit_bytes=VMEM_LIMIT),
        name="rglru",
    )(xt, xt, xt, xt, mod, mod, p["g"], p["wt"], p["wt"], p["aconv"], p["wg"], p["bg"], p["lam"],
      h0)


def _window_sums(padded, w):
    f = padded
    k = 1
    while k < w:
        f = f[:-k] + f[k:]
        k *= 2
    return f


def _window_count(pos, half, size):
    return jnp.minimum(pos + half, size) - jnp.maximum(pos - half, 0)


def _pool_grid_one(v_ref, o_ref, p_scr, w):
    grid_rows = v_ref.shape[0]
    half = w // 2
    tile = v_ref.shape[2:]
    zpad = jnp.zeros((half,) + tile, F32)
    p_scr[0:half] = jnp.zeros((half, GRID_W) + tile, F32)
    p_scr[half + grid_rows:2 * half + grid_rows] = jnp.zeros((half, GRID_W) + tile, F32)

    def row_body(r, carry):
        f = _window_sums(jnp.concatenate([zpad, v_ref[r], zpad], axis=0), w)
        p_scr[half + r] = f[:GRID_W]
        return carry

    lax.fori_loop(0, grid_rows, row_body, 0)

    rpos = lax.broadcasted_iota(jnp.int32, (grid_rows,) + tile, 0)
    rcnt = _window_count(rpos, half, grid_rows)

    def col_body(c, carry):
        f = _window_sums(p_scr[pl.ds(0, grid_rows + 2 * half), c], w)
        cnt = (rcnt * _window_count(c, half, GRID_W)).astype(F32)
        o_ref[:, c] = f[:grid_rows] / cnt - v_ref[:, c]
        return carry

    lax.fori_loop(0, GRID_W, col_body, 0)


def _pool_grid_body(v_ref, o_ref, p_scr):
    gidx = pl.program_id(0)
    for gi, w in enumerate(POOL_WINDOWS):
        @pl.when(gidx == gi)
        def _(w=w):
            _pool_grid_one(v_ref, o_ref, p_scr, w)


def _pool_grid_call(dv):
    n_rows, w = dv.shape
    grid_rows = n_rows // (SUBLANES * GRID_W)
    v4 = dv.reshape(grid_rows, GRID_W, SUBLANES, w)
    blk = (grid_rows, GRID_W, SUBLANES, LANES)
    spec = pl.BlockSpec(blk, lambda gidx: (0, 0, 0, gidx))
    out = pl.pallas_call(
        _pool_grid_body,
        grid=(len(POOL_WINDOWS),),
        in_specs=[spec],
        out_specs=spec,
        out_shape=jax.ShapeDtypeStruct(v4.shape, F32),
        scratch_shapes=[pltpu.VMEM((grid_rows + max(POOL_WINDOWS), GRID_W, SUBLANES, LANES), F32)],
        compiler_params=pltpu.CompilerParams(
            dimension_semantics=("arbitrary",), vmem_limit_bytes=VMEM_LIMIT),
        name="pool_grid",
    )(v4)
    return out.reshape(n_rows, w)


def _pool_seq_body(v_ref, o_ref):
    gidx = pl.program_id(0)
    t = v_ref.shape[0]
    tile = v_ref.shape[1:]
    for gi, w in enumerate(POOL_WINDOWS):
        @pl.when(gidx == gi)
        def _(w=w):
            half = w // 2
            v = v_ref[...]
            zpad = jnp.zeros((half,) + tile, F32)
            f = _window_sums(jnp.concatenate([zpad, v, zpad], axis=0), w)
            pos = lax.broadcasted_iota(jnp.int32, (t,) + tile, 0)
            cnt = _window_count(pos, half, t).astype(F32)
            o_ref[...] = f[:t] / cnt - v


def _pool_seq_call(dv):
    n_rows, w = dv.shape
    t = n_rows // SUBLANES
    v3 = dv.reshape(t, SUBLANES, w)
    spec = pl.BlockSpec((t, SUBLANES, LANES), lambda gidx: (0, 0, gidx))
    out = pl.pallas_call(
        _pool_seq_body,
        grid=(len(POOL_WINDOWS),),
        in_specs=[spec],
        out_specs=spec,
        out_shape=jax.ShapeDtypeStruct(v3.shape, F32),
        compiler_params=pltpu.CompilerParams(
            dimension_semantics=("arbitrary",), vmem_limit_bytes=VMEM_LIMIT),
        name="pool_seq",
    )(v3)
    return out.reshape(n_rows, w)


CONV_FIRST = C_HALO_TOK - C_CONV // 2


def _conv31_block(cb_ref, cconv_ref, cv_ref, lb, row0):
    accs = [None] * CONV_TOK_BLOCK
    taps = {}
    for m in range(CONV_TOK_BLOCK + C_CONV - 1):
        tile = cb_ref[lb, pl.ds(row0 + (CONV_FIRST + m) * SUBLANES, SUBLANES), :]
        for q in range(CONV_TOK_BLOCK):
            k = m - q
            if 0 <= k < C_CONV:
                if k not in taps:
                    taps[k] = cconv_ref[lb, k]
                term = taps[k] * tile
                accs[q] = term if accs[q] is None else accs[q] + term
        taps.pop(m - CONV_TOK_BLOCK + 1, None)
    for q in range(CONV_TOK_BLOCK):
        cv_ref[lb, pl.ds(row0 + q * SUBLANES, SUBLANES), :] = accs[q]


def _conv31_head_fixup(head, cconv_ref, cv_ref, tb):
    reach = C_CONV // 2
    for lb in range(LANE_BLOCKS):
        lanes = slice(lb * LANES, (lb + 1) * LANES)
        for i in range(reach):
            pos = tb - reach + i
            add = None
            for k in range(C_CONV - 1 - i, C_CONV):
                t = k - (C_CONV - 1 - i)
                term = cconv_ref[lb, k] * head[t * SUBLANES:(t + 1) * SUBLANES, lanes]
                add = term if add is None else add + term
            r0 = pos * SUBLANES
            cv_ref[lb, r0:r0 + SUBLANES, :] = cv_ref[lb, r0:r0 + SUBLANES, :] + add


def _mix_body(tb, n, final_norm, xp_ref, xr_ref, sh_ref, sc_ref, gt_ref, g_ref, wt_ref,
              bconv_ref, cconv_ref, lng_ref, lnb_ref, cpw_ref, cpwb_ref, dw_ref, db_ref, dsc_ref,
              wout_ref, yf_ref, yb_ref, p_ref, fg_ref, o_ref,
              hnew_scr, projc_scr, projz_scr, curb_scr, tailb_scr, cbb_scr,
              curc_scr, tailc_scr, cbc_scr, cv_scr):
    j = pl.program_id(0)
    rows = tb * SUBLANES
    hb = B_HALO_TOK * SUBLANES
    hc = C_HALO_TOK * SUBLANES
    d = xp_ref.shape[1]
    gen = j % 2
    blocks_per_lane = tb // CONV_TOK_BLOCK
    blocks_per_tile = LANE_BLOCKS * blocks_per_lane // (N_CONV_TILES + N_GATE_TILES - HALVES)

    @pl.when(j == 0)
    def _():
        projz_scr[1] = jnp.zeros(projz_scr.shape[1:], F32)
        curb_scr[...] = jnp.zeros(curb_scr.shape, F32)
        tailb_scr[...] = jnp.zeros(tailb_scr.shape, F32)
        curc_scr[...] = jnp.zeros(curc_scr.shape, F32)
        tailc_scr[...] = jnp.zeros(tailc_scr.shape, F32)

    hnew_scr[...] = _norm_mod(xp_ref[...], g_ref[...], sc_ref[...], sh_ref[...]).astype(BF16)

    old = 1 - gen

    def slot(s, h):
        return projz_scr[old, s * HALVES + h]

    def wout_rows(branch, h):
        r0 = branch * BRANCH_W + h * MXU_TILE
        return wout_ref[r0:r0 + MXU_TILE, :]

    def half(v, h):
        return v[:, h * MXU_TILE:(h + 1) * MXU_TILE]

    cbc_scr[:, 0:hc, :] = tailc_scr[...]
    cbc_scr[:, hc:hc + rows, :] = curc_scr[...]
    cbc_scr[:, hc + rows:, :] = jnp.zeros((LANE_BLOCKS, hc, LANES), F32)

    def conv_blocks(first_block, i):
        for r in range(blocks_per_tile):
            blk = first_block + i * blocks_per_tile + r
            lb = blk // blocks_per_lane
            row0 = pl.multiple_of((blk % blocks_per_lane) * (CONV_TOK_BLOCK * SUBLANES),
                                  CONV_TOK_BLOCK * SUBLANES)
            _conv31_block(cbc_scr, cconv_ref, cv_scr, lb, row0)

    def conv_tile(i, carry):
        projc_scr[i] = jnp.dot(hnew_scr[...], wt_ref[i], preferred_element_type=F32)
        conv_blocks(0, i)
        return carry

    lax.fori_loop(0, N_CONV_TILES, conv_tile, 0, unroll=TILE_UNROLL)

    def gate_tile(i, carry):
        projz_scr[gen, i] = jnp.dot(hnew_scr[...], wt_ref[N_CONV_TILES + i],
                                    preferred_element_type=F32)
        conv_blocks(N_CONV_TILES * blocks_per_tile, i)
        return carry

    lax.fori_loop(0, N_GATE_TILES - HALVES, gate_tile, 0, unroll=TILE_UNROLL)
    for i in range(N_GATE_TILES - HALVES, N_GATE_TILES):
        projz_scr[gen, i] = jnp.dot(hnew_scr[...], wt_ref[N_CONV_TILES + i],
                                    preferred_element_type=F32)

    has_next = j < n
    ub_new = jnp.concatenate([projc_scr[h] * projc_scr[HALVES + h] for h in range(HALVES)], axis=1)
    uc_new = jnp.concatenate(
        [(0.5 * projc_scr[2 * HALVES + h]) * _tanh_gate(projc_scr[3 * HALVES + h])
         for h in range(HALVES)], axis=1)
    _conv31_head_fixup(jnp.where(has_next, uc_new[0:hc], 0.0), cconv_ref, cv_scr, tb)
    cbb_scr[0:hb] = tailb_scr[...]
    cbb_scr[hb:hb + rows] = curb_scr[...]
    cbb_scr[hb + rows:] = jnp.where(has_next, ub_new[0:hb], 0.0)
    tailb_scr[...] = curb_scr[rows - hb:rows]
    curb_scr[...] = ub_new
    tailc_scr[...] = curc_scr[:, rows - hc:rows, :]
    for lb in range(LANE_BLOCKS):
        curc_scr[lb] = uc_new[:, lb * LANES:(lb + 1) * LANES]

    conv_b = None
    for k in range(B_CONV):
        term = bconv_ref[k:k + 1, :] * cbb_scr[k * SUBLANES:k * SUBLANES + rows, :]
        conv_b = term if conv_b is None else conv_b + term

    cv = jnp.concatenate([cv_scr[lb] for lb in range(LANE_BLOCKS)], axis=1)
    mu = jnp.mean(cv, axis=-1, keepdims=True)
    cen = cv - mu
    var = jnp.mean(cen * cen, axis=-1, keepdims=True)
    ln = cen * lax.rsqrt(var + LN_EPS) * lng_ref[...] + lnb_ref[...]
    yc = jnp.dot(_silu(ln).astype(BF16), cpw_ref[...], preferred_element_type=F32) + cpwb_ref[...]

    ya = yf_ref[...] + yb_ref[...]
    pooled = p_ref[...].astype(BF16)
    out = None
    for h in range(HALVES):
        za, bb, zb, zc, zd = (slot(s, h) for s in range(len(MIX_GATE_SLOTS)))
        yd = (jnp.dot(half(pooled, h), dw_ref[h], preferred_element_type=F32)
              + half(db_ref[...], h)) * half(dsc_ref[...], h)
        cats = (half(ya, h) * (za * _tanh_gate(za)),
                bb * half(conv_b, h) * (zb * _tanh_gate(zb)),
                half(yc, h) * (zc * _tanh_gate(zc)),
                yd * (zd * _tanh_gate(zd)))
        for branch, cat in enumerate(cats):
            term = jnp.dot(cat.astype(BF16), wout_rows(branch, h), preferred_element_type=F32)
            out = term if out is None else out + term

    res = xr_ref[...].reshape(tb, SUBLANES, d) + gt_ref[...][None] * out.reshape(tb, SUBLANES, d)
    res = res.reshape(rows, d)
    if final_norm:
        ms = jnp.mean(res * res, axis=-1, keepdims=True)
        res = res * lax.rsqrt(ms + RMS_EPS) * fg_ref[...]
    o_ref[...] = res


def _mix_call(xt, mod, stream, l, p, yf, yb, pooled, fg, final_norm):
    n_rows, d = xt.shape
    rows = TOK * SUBLANES
    n = n_rows // rows
    w = BRANCH_W
    hb = B_HALO_TOK * SUBLANES
    hc = C_HALO_TOK * SUBLANES
    assert (LANE_BLOCKS * (TOK // CONV_TOK_BLOCK)) % (N_CONV_TILES + N_GATE_TILES - HALVES) == 0
    ahead = lambda j: (jnp.minimum(j, n - 1), 0)
    behind = lambda j: (jnp.maximum(j - 1, 0), 0)
    names = ["g", "wt", "bconv", "cconv", "lng", "lnb", "cpw", "cpwb", "dw", "db", "dsc", "wout"]
    in_specs = (
        [pl.BlockSpec((rows, d), ahead), pl.BlockSpec((rows, d), behind),
         _mod_spec(mod, l, stream, 0), _mod_spec(mod, l, stream, 1), _mod_spec(mod, l, stream, 2)]
        + [_tile_spec(p[k], l, 0, WV_TILE0) if k == "wt" else _layer_spec(p[k], l) for k in names]
        + [pl.BlockSpec((rows, w), behind)] * 3
        + [pl.BlockSpec(fg.shape, lambda j: (0, 0))])
    return pl.pallas_call(
        functools.partial(_mix_body, TOK, n, final_norm),
        grid=(n + 1,),
        in_specs=in_specs,
        out_specs=pl.BlockSpec((rows, d), behind),
        out_shape=jax.ShapeDtypeStruct((n_rows, d), F32),
        scratch_shapes=[pltpu.VMEM((rows, d), BF16),
                        pltpu.VMEM((N_CONV_TILES, rows, MXU_TILE), F32),
                        pltpu.VMEM((2, N_GATE_TILES, rows, MXU_TILE), F32),
                        pltpu.VMEM((rows, w), F32), pltpu.VMEM((hb, w), F32),
                        pltpu.VMEM((rows + 2 * hb, w), F32),
                        pltpu.VMEM((LANE_BLOCKS, rows, LANES), F32),
                        pltpu.VMEM((LANE_BLOCKS, hc, LANES), F32),
                        pltpu.VMEM((LANE_BLOCKS, rows + 2 * hc, LANES), F32),
                        pltpu.VMEM((LANE_BLOCKS, rows, LANES), F32)],
        compiler_params=pltpu.CompilerParams(
            dimension_semantics=("arbitrary",), vmem_limit_bytes=VMEM_LIMIT),
        name="mix",
    )(xt, xt, mod, mod, mod, *[p[k] for k in names], yf, yb, pooled, fg)


def _block_diag(blocks):
    gcount, i, jdim = blocks.shape[-3:]
    eye = jnp.eye(gcount, dtype=blocks.dtype)
    dense = eye[:, None, :, None] * blocks[..., :, :, None, :]
    return dense.reshape(blocks.shape[:-3] + (gcount * i, gcount * jdim))


def _diag_tiles(blocks):
    gcount, i = blocks.shape[-3:-1]
    per_tile = MXU_TILE // i
    grouped = blocks.reshape(blocks.shape[:-3] + (gcount // per_tile, per_tile, i, i))
    return _block_diag(grouped)


def _time_major(a):
    b, t, d = a.shape
    return a.transpose(1, 0, 2).reshape(t * b, d)


def _weight_tiles_body(src_ref, halve_ref, w_ref, o_ref):
    factor = jnp.where(halve_ref[pl.program_id(1)] == 1, 0.5, 1.0)
    o_ref[...] = (w_ref[...] * factor).astype(BF16)


def _weight_tiles(w_in):
    depth, d, _ = w_in.shape
    src = [s * HALVES + h for s in TILE_SLOTS for h in range(HALVES)]
    halve = [int(s in HALVED_SLOTS) for s in TILE_SLOTS for h in range(HALVES)]
    grid_spec = pltpu.PrefetchScalarGridSpec(
        num_scalar_prefetch=2,
        grid=(depth, len(src)),
        in_specs=[pl.BlockSpec((None, d, MXU_TILE), lambda l, t, src_ref, halve_ref: (l, 0, src_ref[t]))],
        out_specs=pl.BlockSpec((None, None, d, MXU_TILE), lambda l, t, src_ref, halve_ref: (l, t, 0, 0)))
    return pl.pallas_call(
        _weight_tiles_body,
        grid_spec=grid_spec,
        out_shape=jax.ShapeDtypeStruct((depth, len(src), d, MXU_TILE), BF16),
        compiler_params=pltpu.CompilerParams(
            dimension_semantics=("arbitrary", "arbitrary"), vmem_limit_bytes=VMEM_LIMIT),
        name="weight_tiles",
    )(jnp.asarray(src, jnp.int32), jnp.asarray(halve, jnp.int32), w_in)


def _prepare_params(norm_g, w_in, w_out, a_conv, a_wr, a_br, a_wi, a_bi, a_lam, b_conv, c_conv,
                    c_ln_g, c_ln_b, c_pw, c_pw_b, d_w, d_b, d_scale):
    depth, d, _ = w_in.shape
    w = BRANCH_W
    gates = jnp.stack([a_wr, a_wi], axis=2)
    wg = (0.5 * _diag_tiles(gates)).reshape(depth, 2, 2 * HALVES, MXU_TILE, MXU_TILE).astype(BF16)
    cconv = c_conv.reshape(depth, C_CONV, LANE_BLOCKS, 1, LANES).transpose(0, 2, 1, 3, 4)
    return dict(
        g=norm_g.reshape(depth, 1, d),
        wt=_weight_tiles(w_in),
        aconv=a_conv, wg=wg,
        bg=(0.5 * jnp.concatenate([a_br, a_bi], axis=-1)).reshape(depth, 2, 1, 2 * w),
        lam=a_lam.reshape(depth, 2, 1, w),
        bconv=b_conv,
        cconv=jnp.broadcast_to(cconv, (depth, LANE_BLOCKS, C_CONV, SUBLANES, LANES)),
        lng=c_ln_g.reshape(depth, 1, w), lnb=c_ln_b.reshape(depth, 1, w),
        cpw=c_pw.astype(BF16), cpwb=c_pw_b.reshape(depth, 1, w),
        dw=_diag_tiles(d_w).astype(BF16), db=d_b.reshape(depth, 1, w),
        dsc=d_scale.reshape(depth, 1, w),
        wout=w_out.astype(BF16))


def kernel(x, c, ctx, c_ctx, mod_w, mod_b, norm_g, w_in, w_out, a_conv, a_wr, a_br, a_wi, a_bi,
           a_lam, b_conv, c_conv, c_ln_g, c_ln_b, c_pw, c_pw_b, d_w, d_b, d_scale, final_g):
    bsz, seq, d = x.shape
    ctx_len = ctx.shape[1]
    depth = w_in.shape[0]
    w = BRANCH_W
    assert bsz == SUBLANES and d == 2 * w
    assert seq % GRID_W == 0 and seq % TOK == 0 and ctx_len % TOK == 0

    xt = _time_major(x)
    ct = _time_major(ctx)
    cc = jnp.concatenate([c, jnp.broadcast_to(c_ctx[None, :], (SUBLANES, d))], axis=0)
    mod = _modulation(cc, mod_w, mod_b).reshape(depth, 2, SUBLANES, 3 * d)
    p = _prepare_params(norm_g, w_in, w_out, a_conv, a_wr, a_br, a_wi, a_bi, a_lam, b_conv, c_conv,
                        c_ln_g, c_ln_b, c_pw, c_pw_b, d_w, d_b, d_scale)
    fg = final_g.reshape(1, d)
    zeros_state = jnp.zeros((2, SUBLANES, w), F32)

    for l in range(depth):
        last = l == depth - 1
        yf, yb, dv, state_k = _rglru_call(ct, mod, 1, l, p, zeros_state)
        if not last:
            ct = _mix_call(ct, mod, 1, l, p, yf, yb, _pool_seq_call(dv), fg, False)
        yf, yb, dv, _ = _rglru_call(xt, mod, 0, l, p, state_k)
        xt = _mix_call(xt, mod, 0, l, p, yf, yb, _pool_grid_call(dv), fg, last)

    return xt.reshape(seq, bsz, d).transpose(1, 0, 2)
```

```python
import functools

import jax
import jax.numpy as jnp
from jax import lax
from jax.experimental import pallas as pl
from jax.experimental.pallas import tpu as pltpu

F32 = jnp.float32
BF16 = jnp.bfloat16

SUBLANES = 8
LANES = 128
MXU_TILE = 256
BRANCH_W = 512
A_CONV = 4
B_CONV = 3
C_CONV = 31
RG_C = 8.0
GRID_W = 64
POOL_WINDOWS = (2, 4, 8, 16)
RMS_EPS = 1e-6
LN_EPS = 1e-5

A_HALO_TOK = 4
B_HALO_TOK = B_CONV // 2
C_HALO_TOK = 16
CONV_TOK_BLOCK = 8
TOK = 64
TILE_UNROLL = 8
VMEM_LIMIT = 60 * 1024 * 1024

LANE_BLOCKS = BRANCH_W // LANES
HALVES = BRANCH_W // MXU_TILE
MIX_CONV_SLOTS = (3, 4, 6, 7)
MIX_GATE_SLOTS = (1, 2, 5, 8, 10)
N_CONV_TILES = len(MIX_CONV_SLOTS) * HALVES
N_GATE_TILES = len(MIX_GATE_SLOTS) * HALVES
TILE_SLOTS = MIX_CONV_SLOTS + MIX_GATE_SLOTS + (0, 9)
WV_TILE0 = N_CONV_TILES + N_GATE_TILES
WD_TILE0 = WV_TILE0 + HALVES
HALVED_SLOTS = (1, 5, 7, 8, 10)


def _tanh_gate(v):
    return 1.0 + jnp.tanh(v)


def _silu(v):
    return (0.5 * v) * _tanh_gate(0.5 * v)


def _norm_mod(xv, g, sc, sh):
    rows, d = xv.shape
    ms = jnp.mean(xv * xv, axis=-1, keepdims=True)
    y = xv * lax.rsqrt(ms + RMS_EPS) * g
    y3 = y.reshape(rows // SUBLANES, SUBLANES, d)
    hmod = y3 * (1.0 + sc)[None] + sh[None]
    return hmod.reshape(rows, d)


def _layer_spec(arr, l):
    zeros = (0,) * (arr.ndim - 1)
    return pl.BlockSpec((None,) + arr.shape[1:], lambda j: (l,) + zeros)


def _tile_spec(wt, l, first, count):
    assert first % count == 0
    return pl.BlockSpec((None, count) + wt.shape[2:], lambda j: (l, first // count, 0, 0))


def _mod_spec(mod, l, stream, part):
    d = mod.shape[3] // 3
    return pl.BlockSpec((None, None, SUBLANES, d), lambda j: (l, stream, 0, part))


def _mod_body(c_ref, w_ref, b_ref, o_ref):
    s = _silu(c_ref[...])
    o_ref[0] = jnp.dot(s, w_ref[0], preferred_element_type=F32,
                       precision=lax.Precision.HIGHEST) + b_ref[0]


def _modulation(cc, mod_w, mod_b):
    depth, d, d3 = mod_w.shape
    nj = d3 // d
    rows = cc.shape[0]
    return pl.pallas_call(
        _mod_body,
        grid=(depth, nj),
        in_specs=[pl.BlockSpec((rows, d), lambda l, j: (0, 0)),
                  pl.BlockSpec((1, d, d), lambda l, j: (l, 0, j)),
                  pl.BlockSpec((1, 1, d), lambda l, j: (l, 0, j))],
        out_specs=pl.BlockSpec((1, rows, d), lambda l, j: (l, 0, j)),
        out_shape=jax.ShapeDtypeStruct((depth, rows, d3), F32),
        compiler_params=pltpu.CompilerParams(
            dimension_semantics=("arbitrary", "arbitrary"), vmem_limit_bytes=VMEM_LIMIT),
        name="modulation",
    )(cc, mod_w, mod_b.reshape(depth, 1, d3))


def _rglru_coeffs(xc, wg_ref, bg_ref, lam_ref, n, a_scr, b_scr):
    nl = -lam_ref[n]
    softplus = jnp.maximum(nl, 0.0) + jnp.log1p(jnp.exp(-jnp.abs(nl)))
    half_c = (-0.5 * RG_C) * softplus
    xb = xc.astype(BF16)
    bg = bg_ref[n]
    w = xc.shape[1]
    for hb in range(HALVES):
        cols = slice(hb * MXU_TILE, (hb + 1) * MXU_TILE)
        icols = slice(w + hb * MXU_TILE, w + (hb + 1) * MXU_TILE)
        xh = xb[:, cols]
        gr = jnp.dot(xh, wg_ref[n, hb], preferred_element_type=F32) + bg[:, cols]
        gi = jnp.dot(xh, wg_ref[n, HALVES + hb], preferred_element_type=F32) + bg[:, icols]
        log_a = half_c[:, cols] * _tanh_gate(gr)
        a = jnp.exp(log_a)
        s = -jnp.tanh(log_a) * (1.0 + a * a)
        root = jnp.where(s > 0.0, s * lax.rsqrt(s), 0.0)
        a_scr[n, :, cols] = a
        b_scr[n, :, cols] = root * (_tanh_gate(gi) * (0.5 * xc[:, cols]))


def _tile_dot(lhs, w_ref):
    return jnp.concatenate([jnp.dot(lhs, w_ref[t], preferred_element_type=F32)
                            for t in range(w_ref.shape[0])], axis=1)


def _rglru_body(ta, xf_ref, xfp_ref, xb_ref, xbn_ref, sh_ref, sc_ref, g_ref, wv_ref, wd_ref,
                aconv_ref, wg_ref, bg_ref, lam_ref, h0_ref,
                yf_ref, yb_ref, dv_ref, hfin_ref, st_ref, a_scr, b_scr):
    j = pl.program_id(0)
    rows = ta * SUBLANES
    halo = A_HALO_TOK * SUBLANES
    w = BRANCH_W

    @pl.when(j == 0)
    def _():
        st_ref[...] = h0_ref[...]

    g = g_ref[...]
    sc = sc_ref[...]
    sh = sh_ref[...]
    inner = j > 0

    hf = _norm_mod(jnp.concatenate([xfp_ref[...], xf_ref[...]], axis=0), g, sc, sh).astype(BF16)
    va = _tile_dot(hf, wv_ref)
    dv_ref[...] = _tile_dot(hf[halo:], wd_ref)
    va = jnp.concatenate([jnp.where(inner, va[:halo], 0.0), va[halo:]], axis=0)
    xc = None
    for k in range(A_CONV):
        off = (A_HALO_TOK - (A_CONV - 1) + k) * SUBLANES
        term = aconv_ref[0, k:k + 1, :] * va[off:off + rows]
        xc = term if xc is None else xc + term
    _rglru_coeffs(xc, wg_ref, bg_ref, lam_ref, 0, a_scr, b_scr)

    hb = _norm_mod(jnp.concatenate([xb_ref[...], xbn_ref[...]], axis=0), g, sc, sh).astype(BF16)
    vb = _tile_dot(hb, wv_ref)
    vb = jnp.concatenate([vb[:rows], jnp.where(inner, vb[rows:], 0.0)], axis=0)
    xc = None
    for k in range(A_CONV):
        off = k * SUBLANES
        term = aconv_ref[1, k:k + 1, :] * vb[off:off + rows]
        xc = term if xc is None else xc + term
    _rglru_coeffs(xc, wg_ref, bg_ref, lam_ref, 1, a_scr, b_scr)

    def step(i, carry):
        h_f, h_b = carry
        rf = pl.multiple_of(i * SUBLANES, SUBLANES)
        rb = pl.multiple_of((ta - 1 - i) * SUBLANES, SUBLANES)
        h_f = a_scr[0, pl.ds(rf, SUBLANES), :] * h_f + b_scr[0, pl.ds(rf, SUBLANES), :]
        yf_ref[pl.ds(rf, SUBLANES), :] = h_f
        h_b = a_scr[1, pl.ds(rb, SUBLANES), :] * h_b + b_scr[1, pl.ds(rb, SUBLANES), :]
        yb_ref[pl.ds(rb, SUBLANES), :] = h_b
        return h_f, h_b

    h_f, h_b = lax.fori_loop(0, ta, step, (st_ref[0], st_ref[1]), unroll=8)
    st_ref[0] = h_f
    st_ref[1] = h_b
    hfin_ref[0] = h_f
    hfin_ref[1] = h_b


def _rglru_call(xt, mod, stream, l, p, h0):
    n_rows, d = xt.shape
    rows = TOK * SUBLANES
    halo = A_HALO_TOK * SUBLANES
    n = n_rows // rows
    per = rows // halo
    last_halo = n_rows // halo - 1
    w = BRANCH_W
    in_specs = [
        pl.BlockSpec((rows, d), lambda j: (j, 0)),
        pl.BlockSpec((halo, d), lambda j: (jnp.maximum(j * per - 1, 0), 0)),
        pl.BlockSpec((rows, d), lambda j: (n - 1 - j, 0)),
        pl.BlockSpec((halo, d), lambda j: (jnp.minimum((n - j) * per, last_halo), 0)),
        _mod_spec(mod, l, stream, 0), _mod_spec(mod, l, stream, 1),
        _layer_spec(p["g"], l), _tile_spec(p["wt"], l, WV_TILE0, HALVES),
        _tile_spec(p["wt"], l, WD_TILE0, HALVES), _layer_spec(p["aconv"], l),
        _layer_spec(p["wg"], l), _layer_spec(p["bg"], l), _layer_spec(p["lam"], l),
        pl.BlockSpec((2, SUBLANES, w), lambda j: (0, 0, 0)),
    ]
    out_specs = [
        pl.BlockSpec((rows, w), lambda j: (j, 0)),
        pl.BlockSpec((rows, w), lambda j: (n - 1 - j, 0)),
        pl.BlockSpec((rows, w), lambda j: (j, 0)),
        pl.BlockSpec((2, SUBLANES, w), lambda j: (0, 0, 0)),
    ]
    out_shape = [jax.ShapeDtypeStruct((n_rows, w), F32)] * 3 + [
        jax.ShapeDtypeStruct((2, SUBLANES, w), F32)]
    return pl.pallas_call(
        functools.partial(_rglru_body, TOK),
        grid=(n,),
        in_specs=in_specs,
        out_specs=out_specs,
        out_shape=out_shape,
        scratch_shapes=[pltpu.VMEM((2, SUBLANES, w), F32),
                        pltpu.VMEM((2, rows, w), F32),
                        pltpu.VMEM((2, rows, w), F32)],
        compiler_params=pltpu.CompilerParams(
            dimension_semantics=("arbitrary",), vmem_limit_bytes=VMEM_LIMIT),
        name="rglru",
    )(xt, xt, xt, xt, mod, mod, p["g"], p["wt"], p["wt"], p["aconv"], p["wg"], p["bg"], p["lam"],
      h0)


def _window_sums(padded, w):
    f = padded
    k = 1
    while k < w:
        f = f[:-k] + f[k:]
        k *= 2
    return f


def _window_count(pos, half, size):
    return jnp.minimum(pos + half, size) - jnp.maximum(pos - half, 0)


def _pool_grid_one(v_ref, o_ref, p_scr, w):
    grid_rows = v_ref.shape[0]
    half = w // 2
    tile = v_ref.shape[2:]
    zpad = jnp.zeros((half,) + tile, F32)
    p_scr[0:half] = jnp.zeros((half, GRID_W) + tile, F32)
    p_scr[half + grid_rows:2 * half + grid_rows] = jnp.zeros((half, GRID_W) + tile, F32)

    def row_body(r, carry):
        f = _window_sums(jnp.concatenate([zpad, v_ref[r], zpad], axis=0), w)
        p_scr[half + r] = f[:GRID_W]
        return carry

    lax.fori_loop(0, grid_rows, row_body, 0)

    rpos = lax.broadcasted_iota(jnp.int32, (grid_rows,) + tile, 0)
    rcnt = _window_count(rpos, half, grid_rows)

    def col_body(c, carry):
        f = _window_sums(p_scr[pl.ds(0, grid_rows + 2 * half), c], w)
        cnt = (rcnt * _window_count(c, half, GRID_W)).astype(F32)
        o_ref[:, c] = f[:grid_rows] / cnt - v_ref[:, c]
        return carry

    lax.fori_loop(0, GRID_W, col_body, 0)


def _pool_grid_body(v_ref, o_ref, p_scr):
    gidx = pl.program_id(0)
    for gi, w in enumerate(POOL_WINDOWS):
        @pl.when(gidx == gi)
        def _(w=w):
            _pool_grid_one(v_ref, o_ref, p_scr, w)


def _pool_grid_call(dv):
    n_rows, w = dv.shape
    grid_rows = n_rows // (SUBLANES * GRID_W)
    v4 = dv.reshape(grid_rows, GRID_W, SUBLANES, w)
    blk = (grid_rows, GRID_W, SUBLANES, LANES)
    spec = pl.BlockSpec(blk, lambda gidx: (0, 0, 0, gidx))
    out = pl.pallas_call(
        _pool_grid_body,
        grid=(len(POOL_WINDOWS),),
        in_specs=[spec],
        out_specs=spec,
        out_shape=jax.ShapeDtypeStruct(v4.shape, F32),
        scratch_shapes=[pltpu.VMEM((grid_rows + max(POOL_WINDOWS), GRID_W, SUBLANES, LANES), F32)],
        compiler_params=pltpu.CompilerParams(
            dimension_semantics=("arbitrary",), vmem_limit_bytes=VMEM_LIMIT),
        name="pool_grid",
    )(v4)
    return out.reshape(n_rows, w)


def _pool_seq_body(v_ref, o_ref):
    gidx = pl.program_id(0)
    t = v_ref.shape[0]
    tile = v_ref.shape[1:]
    for gi, w in enumerate(POOL_WINDOWS):
        @pl.when(gidx == gi)
        def _(w=w):
            half = w // 2
            v = v_ref[...]
            zpad = jnp.zeros((half,) + tile, F32)
            f = _window_sums(jnp.concatenate([zpad, v, zpad], axis=0), w)
            pos = lax.broadcasted_iota(jnp.int32, (t,) + tile, 0)
            cnt = _window_count(pos, half, t).astype(F32)
            o_ref[...] = f[:t] / cnt - v


def _pool_seq_call(dv):
    n_rows, w = dv.shape
    t = n_rows // SUBLANES
    v3 = dv.reshape(t, SUBLANES, w)
    spec = pl.BlockSpec((t, SUBLANES, LANES), lambda gidx: (0, 0, gidx))
    out = pl.pallas_call(
        _pool_seq_body,
        grid=(len(POOL_WINDOWS),),
        in_specs=[spec],
        out_specs=spec,
        out_shape=jax.ShapeDtypeStruct(v3.shape, F32),
        compiler_params=pltpu.CompilerParams(
            dimension_semantics=("arbitrary",), vmem_limit_bytes=VMEM_LIMIT),
        name="pool_seq",
    )(v3)
    return out.reshape(n_rows, w)


CONV_FIRST = C_HALO_TOK - C_CONV // 2


def _conv31_block(cb_ref, cconv_ref, cv_ref, lb, row0):
    accs = [None] * CONV_TOK_BLOCK
    taps = {}
    for m in range(CONV_TOK_BLOCK + C_CONV - 1):
        tile = cb_ref[lb, pl.ds(row0 + (CONV_FIRST + m) * SUBLANES, SUBLANES), :]
        for q in range(CONV_TOK_BLOCK):
            k = m - q
            if 0 <= k < C_CONV:
                if k not in taps:
                    taps[k] = cconv_ref[lb, k]
                term = taps[k] * tile
                accs[q] = term if accs[q] is None else accs[q] + term
        taps.pop(m - CONV_TOK_BLOCK + 1, None)
    for q in range(CONV_TOK_BLOCK):
        cv_ref[lb, pl.ds(row0 + q * SUBLANES, SUBLANES), :] = accs[q]


def _conv31_head_fixup(head, cconv_ref, cv_ref, tb):
    reach = C_CONV // 2
    for lb in range(LANE_BLOCKS):
        lanes = slice(lb * LANES, (lb + 1) * LANES)
        for i in range(reach):
            pos = tb - reach + i
            add = None
            for k in range(C_CONV - 1 - i, C_CONV):
                t = k - (C_CONV - 1 - i)
                term = cconv_ref[lb, k] * head[t * SUBLANES:(t + 1) * SUBLANES, lanes]
                add = term if add is None else add + term
            r0 = pos * SUBLANES
            cv_ref[lb, r0:r0 + SUBLANES, :] = cv_ref[lb, r0:r0 + SUBLANES, :] + add


def _mix_body(tb, n, final_norm, xp_ref, xr_ref, sh_ref, sc_ref, gt_ref, g_ref, wt_ref,
              bconv_ref, cconv_ref, lng_ref, lnb_ref, cpw_ref, cpwb_ref, dw_ref, db_ref, dsc_ref,
              wout_ref, yf_ref, yb_ref, p_ref, fg_ref, o_ref,
              hnew_scr, projc_scr, projz_scr, curb_scr, tailb_scr, cbb_scr,
              curc_scr, tailc_scr, cbc_scr, cv_scr):
    j = pl.program_id(0)
    rows = tb * SUBLANES
    hb = B_HALO_TOK * SUBLANES
    hc = C_HALO_TOK * SUBLANES
    d = xp_ref.shape[1]
    gen = j % 2
    new_z = projz_scr.at[gen]
    old_z = projz_scr.at[1 - gen]
    blocks_per_lane = tb // CONV_TOK_BLOCK
    blocks_per_tile = LANE_BLOCKS * blocks_per_lane // (N_CONV_TILES + N_GATE_TILES - HALVES)

    @pl.when(j == 0)
    def _():
        projz_scr[1] = jnp.zeros(projz_scr.shape[1:], F32)
        curb_scr[...] = jnp.zeros(curb_scr.shape, F32)
        tailb_scr[...] = jnp.zeros(tailb_scr.shape, F32)
        curc_scr[...] = jnp.zeros(curc_scr.shape, F32)
        tailc_scr[...] = jnp.zeros(tailc_scr.shape, F32)

    hnew_scr[...] = _norm_mod(xp_ref[...], g_ref[...], sc_ref[...], sh_ref[...]).astype(BF16)

    def slot(s, h):
        return old_z[s * HALVES + h]

    def wout_rows(branch, h):
        r0 = branch * BRANCH_W + h * MXU_TILE
        return wout_ref[r0:r0 + MXU_TILE, :]

    def half(v, h):
        return v[:, h * MXU_TILE:(h + 1) * MXU_TILE]

    cbc_scr[:, 0:hc, :] = tailc_scr[...]
    cbc_scr[:, hc:hc + rows, :] = curc_scr[...]
    cbc_scr[:, hc + rows:, :] = jnp.zeros((LANE_BLOCKS, hc, LANES), F32)

    def conv_blocks(first_block, i):
        for r in range(blocks_per_tile):
            blk = first_block + i * blocks_per_tile + r
            lb = blk // blocks_per_lane
            row0 = pl.multiple_of((blk % blocks_per_lane) * (CONV_TOK_BLOCK * SUBLANES),
                                  CONV_TOK_BLOCK * SUBLANES)
            _conv31_block(cbc_scr, cconv_ref, cv_scr, lb, row0)

    def conv_tile(i, carry):
        projc_scr[i] = jnp.dot(hnew_scr[...], wt_ref[i], preferred_element_type=F32)
        conv_blocks(0, i)
        return carry

    lax.fori_loop(0, N_CONV_TILES, conv_tile, 0, unroll=TILE_UNROLL)

    def gate_tile(i, carry):
        new_z[i] = jnp.dot(hnew_scr[...], wt_ref[N_CONV_TILES + i], preferred_element_type=F32)
        conv_blocks(N_CONV_TILES * blocks_per_tile, i)
        return carry

    lax.fori_loop(0, N_GATE_TILES - HALVES, gate_tile, 0, unroll=TILE_UNROLL)
    for i in range(N_GATE_TILES - HALVES, N_GATE_TILES):
        new_z[i] = jnp.dot(hnew_scr[...], wt_ref[N_CONV_TILES + i], preferred_element_type=F32)

    has_next = j < n
    ub_new = jnp.concatenate([projc_scr[h] * projc_scr[HALVES + h] for h in range(HALVES)], axis=1)
    uc_new = jnp.concatenate(
        [(0.5 * projc_scr[2 * HALVES + h]) * _tanh_gate(projc_scr[3 * HALVES + h])
         for h in range(HALVES)], axis=1)
    _conv31_head_fixup(jnp.where(has_next, uc_new[0:hc], 0.0), cconv_ref, cv_scr, tb)
    cbb_scr[0:hb] = tailb_scr[...]
    cbb_scr[hb:hb + rows] = curb_scr[...]
    cbb_scr[hb + rows:] = jnp.where(has_next, ub_new[0:hb], 0.0)
    tailb_scr[...] = curb_scr[rows - hb:rows]
    curb_scr[...] = ub_new
    tailc_scr[...] = curc_scr[:, rows - hc:rows, :]
    for lb in range(LANE_BLOCKS):
        curc_scr[lb] = uc_new[:, lb * LANES:(lb + 1) * LANES]

    conv_b = None
    for k in range(B_CONV):
        term = bconv_ref[k:k + 1, :] * cbb_scr[k * SUBLANES:k * SUBLANES + rows, :]
        conv_b = term if conv_b is None else conv_b + term

    cv = jnp.concatenate([cv_scr[lb] for lb in range(LANE_BLOCKS)], axis=1)
    mu = jnp.mean(cv, axis=-1, keepdims=True)
    cen = cv - mu
    var = jnp.mean(cen * cen, axis=-1, keepdims=True)
    ln = cen * lax.rsqrt(var + LN_EPS) * lng_ref[...] + lnb_ref[...]
    yc = jnp.dot(_silu(ln).astype(BF16), cpw_ref[...], preferred_element_type=F32) + cpwb_ref[...]

    ya = yf_ref[...] + yb_ref[...]
    pooled = p_ref[...].astype(BF16)
    out = None
    for h in range(HALVES):
        za, bb, zb, zc, zd = (slot(s, h) for s in range(len(MIX_GATE_SLOTS)))
        yd = (jnp.dot(half(pooled, h), dw_ref[h], preferred_element_type=F32)
              + half(db_ref[...], h)) * half(dsc_ref[...], h)
        cats = (half(ya, h) * (za * _tanh_gate(za)),
                bb * half(conv_b, h) * (zb * _tanh_gate(zb)),
                half(yc, h) * (zc * _tanh_gate(zc)),
                yd * (zd * _tanh_gate(zd)))
        for branch, cat in enumerate(cats):
            term = jnp.dot(cat.astype(BF16), wout_rows(branch, h), preferred_element_type=F32)
            out = term if out is None else out + term

    res = xr_ref[...].reshape(tb, SUBLANES, d) + gt_ref[...][None] * out.reshape(tb, SUBLANES, d)
    res = res.reshape(rows, d)
    if final_norm:
        ms = jnp.mean(res * res, axis=-1, keepdims=True)
        res = res * lax.rsqrt(ms + RMS_EPS) * fg_ref[...]
    o_ref[...] = res


def _mix_call(xt, mod, stream, l, p, yf, yb, pooled, fg, final_norm):
    n_rows, d = xt.shape
    rows = TOK * SUBLANES
    n = n_rows // rows
    w = BRANCH_W
    hb = B_HALO_TOK * SUBLANES
    hc = C_HALO_TOK * SUBLANES
    assert (LANE_BLOCKS * (TOK // CONV_TOK_BLOCK)) % (N_CONV_TILES + N_GATE_TILES - HALVES) == 0
    ahead = lambda j: (jnp.minimum(j, n - 1), 0)
    behind = lambda j: (jnp.maximum(j - 1, 0), 0)
    names = ["g", "wt", "bconv", "cconv", "lng", "lnb", "cpw", "cpwb", "dw", "db", "dsc", "wout"]
    in_specs = (
        [pl.BlockSpec((rows, d), ahead), pl.BlockSpec((rows, d), behind),
         _mod_spec(mod, l, stream, 0), _mod_spec(mod, l, stream, 1), _mod_spec(mod, l, stream, 2)]
        + [_tile_spec(p[k], l, 0, WV_TILE0) if k == "wt" else _layer_spec(p[k], l) for k in names]
        + [pl.BlockSpec((rows, w), behind)] * 3
        + [pl.BlockSpec(fg.shape, lambda j: (0, 0))])
    return pl.pallas_call(
        functools.partial(_mix_body, TOK, n, final_norm),
        grid=(n + 1,),
        in_specs=in_specs,
        out_specs=pl.BlockSpec((rows, d), behind),
        out_shape=jax.ShapeDtypeStruct((n_rows, d), F32),
        scratch_shapes=[pltpu.VMEM((rows, d), BF16),
                        pltpu.VMEM((N_CONV_TILES, rows, MXU_TILE), F32),
                        pltpu.VMEM((2, N_GATE_TILES, rows, MXU_TILE), F32),
                        pltpu.VMEM((rows, w), F32), pltpu.VMEM((hb, w), F32),
                        pltpu.VMEM((rows + 2 * hb, w), F32),
                        pltpu.VMEM((LANE_BLOCKS, rows, LANES), F32),
                        pltpu.VMEM((LANE_BLOCKS, hc, LANES), F32),
                        pltpu.VMEM((LANE_BLOCKS, rows + 2 * hc, LANES), F32),
                        pltpu.VMEM((LANE_BLOCKS, rows, LANES), F32)],
        compiler_params=pltpu.CompilerParams(
            dimension_semantics=("arbitrary",), vmem_limit_bytes=VMEM_LIMIT),
        name="mix",
    )(xt, xt, mod, mod, mod, *[p[k] for k in names], yf, yb, pooled, fg)


def _block_diag(blocks):
    gcount, i, jdim = blocks.shape[-3:]
    eye = jnp.eye(gcount, dtype=blocks.dtype)
    dense = eye[:, None, :, None] * blocks[..., :, :, None, :]
    return dense.reshape(blocks.shape[:-3] + (gcount * i, gcount * jdim))


def _diag_tiles(blocks):
    gcount, i = blocks.shape[-3:-1]
    per_tile = MXU_TILE // i
    grouped = blocks.reshape(blocks.shape[:-3] + (gcount // per_tile, per_tile, i, i))
    return _block_diag(grouped)


def _time_major(a):
    b, t, d = a.shape
    return a.transpose(1, 0, 2).reshape(t * b, d)


def _weight_tiles_body(src_ref, halve_ref, w_ref, o_ref):
    factor = jnp.where(halve_ref[pl.program_id(1)] == 1, 0.5, 1.0)
    for h in range(HALVES):
        o_ref[h] = (w_ref[:, h * MXU_TILE:(h + 1) * MXU_TILE] * factor).astype(BF16)


def _weight_tiles(w_in):
    depth, d, _ = w_in.shape
    src = list(TILE_SLOTS)
    halve = [int(s in HALVED_SLOTS) for s in TILE_SLOTS]
    grid_spec = pltpu.PrefetchScalarGridSpec(
        num_scalar_prefetch=2,
        grid=(depth, len(src)),
        in_specs=[pl.BlockSpec((None, d, BRANCH_W), lambda l, t, src_ref, halve_ref: (l, 0, src_ref[t]))],
        out_specs=pl.BlockSpec((None, HALVES, d, MXU_TILE),
                               lambda l, t, src_ref, halve_ref: (l, t, 0, 0)))
    return pl.pallas_call(
        _weight_tiles_body,
        grid_spec=grid_spec,
        out_shape=jax.ShapeDtypeStruct((depth, len(src) * HALVES, d, MXU_TILE), BF16),
        compiler_params=pltpu.CompilerParams(
            dimension_semantics=("arbitrary", "arbitrary"), vmem_limit_bytes=VMEM_LIMIT),
        name="weight_tiles",
    )(jnp.asarray(src, jnp.int32), jnp.asarray(halve, jnp.int32), w_in)


def _prepare_params(norm_g, w_in, w_out, a_conv, a_wr, a_br, a_wi, a_bi, a_lam, b_conv, c_conv,
                    c_ln_g, c_ln_b, c_pw, c_pw_b, d_w, d_b, d_scale):
    depth, d, _ = w_in.shape
    w = BRANCH_W
    gates = jnp.stack([a_wr, a_wi], axis=2)
    wg = (0.5 * _diag_tiles(gates)).reshape(depth, 2, 2 * HALVES, MXU_TILE, MXU_TILE).astype(BF16)
    cconv = c_conv.reshape(depth, C_CONV, LANE_BLOCKS, 1, LANES).transpose(0, 2, 1, 3, 4)
    return dict(
        g=norm_g.reshape(depth, 1, d),
        wt=_weight_tiles(w_in),
        aconv=a_conv, wg=wg,
        bg=(0.5 * jnp.concatenate([a_br, a_bi], axis=-1)).reshape(depth, 2, 1, 2 * w),
        lam=a_lam.reshape(depth, 2, 1, w),
        bconv=b_conv,
        cconv=jnp.broadcast_to(cconv, (depth, LANE_BLOCKS, C_CONV, SUBLANES, LANES)),
        lng=c_ln_g.reshape(depth, 1, w), lnb=c_ln_b.reshape(depth, 1, w),
        cpw=c_pw.astype(BF16), cpwb=c_pw_b.reshape(depth, 1, w),
        dw=_diag_tiles(d_w).astype(BF16), db=d_b.reshape(depth, 1, w),
        dsc=d_scale.reshape(depth, 1, w),
        wout=w_out.astype(BF16))


def kernel(x, c, ctx, c_ctx, mod_w, mod_b, norm_g, w_in, w_out, a_conv, a_wr, a_br, a_wi, a_bi,
           a_lam, b_conv, c_conv, c_ln_g, c_ln_b, c_pw, c_pw_b, d_w, d_b, d_scale, final_g):
    bsz, seq, d = x.shape
    ctx_len = ctx.shape[1]
    depth = w_in.shape[0]
    w = BRANCH_W
    assert bsz == SUBLANES and d == 2 * w
    assert seq % GRID_W == 0 and seq % TOK == 0 and ctx_len % TOK == 0

    xt = _time_major(x)
    ct = _time_major(ctx)
    cc = jnp.concatenate([c, jnp.broadcast_to(c_ctx[None, :], (SUBLANES, d))], axis=0)
    mod = _modulation(cc, mod_w, mod_b).reshape(depth, 2, SUBLANES, 3 * d)
    p = _prepare_params(norm_g, w_in, w_out, a_conv, a_wr, a_br, a_wi, a_bi, a_lam, b_conv, c_conv,
                        c_ln_g, c_ln_b, c_pw, c_pw_b, d_w, d_b, d_scale)
    fg = final_g.reshape(1, d)
    zeros_state = jnp.zeros((2, SUBLANES, w), F32)

    for l in range(depth):
        last = l == depth - 1
        yf, yb, dv, state_k = _rglru_call(ct, mod, 1, l, p, zeros_state)
        if not last:
            ct = _mix_call(ct, mod, 1, l, p, yf, yb, _pool_seq_call(dv), fg, False)
        yf, yb, dv, _ = _rglru_call(xt, mod, 0, l, p, state_k)
        xt = _mix_call(xt, mod, 0, l, p, yf, yb, _pool_grid_call(dv), fg, last)

    return xt.reshape(seq, bsz, d).transpose(1, 0, 2)
```

```python
import functools

import jax
import jax.numpy as jnp
from jax import lax
from jax.experimental import pallas as pl
from jax.experimental.pallas import tpu as pltpu

F32 = jnp.float32
BF16 = jnp.bfloat16

SUBLANES = 8
LANES = 128
MXU_TILE = 256
BRANCH_W = 512
A_CONV = 4
B_CONV = 3
C_CONV = 31
RG_C = 8.0
GRID_W = 64
POOL_WINDOWS = (2, 4, 8, 16)
RMS_EPS = 1e-6
LN_EPS = 1e-5

A_HALO_TOK = 4
B_HALO_TOK = B_CONV // 2
C_HALO_TOK = 16
CONV_TOK_BLOCK = 8
TOK = 64
TILE_UNROLL = 8
VMEM_LIMIT = 60 * 1024 * 1024

LANE_BLOCKS = BRANCH_W // LANES
HALVES = BRANCH_W // MXU_TILE
MIX_CONV_SLOTS = (3, 4, 6, 7)
MIX_GATE_SLOTS = (1, 2, 5, 8, 10)
N_CONV_TILES = len(MIX_CONV_SLOTS) * HALVES
N_GATE_TILES = len(MIX_GATE_SLOTS) * HALVES
TILE_SLOTS = MIX_CONV_SLOTS + MIX_GATE_SLOTS + (0, 9)
WV_TILE0 = N_CONV_TILES + N_GATE_TILES
WD_TILE0 = WV_TILE0 + HALVES
HALVED_SLOTS = (1, 5, 7, 8, 10)


def _tanh_gate(v):
    return 1.0 + jnp.tanh(v)


def _silu(v):
    return (0.5 * v) * _tanh_gate(0.5 * v)


def _norm_mod(xv, g, sc, sh):
    rows, d = xv.shape
    ms = jnp.mean(xv * xv, axis=-1, keepdims=True)
    y = xv * lax.rsqrt(ms + RMS_EPS) * g
    y3 = y.reshape(rows // SUBLANES, SUBLANES, d)
    hmod = y3 * (1.0 + sc)[None] + sh[None]
    return hmod.reshape(rows, d)


def _layer_spec(arr, l):
    zeros = (0,) * (arr.ndim - 1)
    return pl.BlockSpec((None,) + arr.shape[1:], lambda j: (l,) + zeros,
                        pipeline_mode=pl.Buffered(1))


def _tile_spec(wt, l, first, count):
    assert first % count == 0
    return pl.BlockSpec((None, count) + wt.shape[2:], lambda j: (l, first // count, 0, 0),
                        pipeline_mode=pl.Buffered(1))


def _mod_spec(mod, l, stream, part):
    d = mod.shape[3] // 3
    return pl.BlockSpec((None, None, SUBLANES, d), lambda j: (l, stream, 0, part))


def _mod_body(c_ref, w_ref, b_ref, o_ref):
    s = _silu(c_ref[...])
    o_ref[0] = jnp.dot(s, w_ref[0], preferred_element_type=F32,
                       precision=lax.Precision.HIGHEST) + b_ref[0]


def _modulation(cc, mod_w, mod_b):
    depth, d, d3 = mod_w.shape
    nj = d3 // d
    rows = cc.shape[0]
    return pl.pallas_call(
        _mod_body,
        grid=(depth, nj),
        in_specs=[pl.BlockSpec((rows, d), lambda l, j: (0, 0)),
                  pl.BlockSpec((1, d, d), lambda l, j: (l, 0, j)),
                  pl.BlockSpec((1, 1, d), lambda l, j: (l, 0, j))],
        out_specs=pl.BlockSpec((1, rows, d), lambda l, j: (l, 0, j)),
        out_shape=jax.ShapeDtypeStruct((depth, rows, d3), F32),
        compiler_params=pltpu.CompilerParams(
            dimension_semantics=("arbitrary", "arbitrary"), vmem_limit_bytes=VMEM_LIMIT),
        name="modulation",
    )(cc, mod_w, mod_b.reshape(depth, 1, d3))


def _rglru_coeffs(xc, wg_ref, bg_ref, lam_ref, n, a_scr, b_scr):
    nl = -lam_ref[n]
    softplus = jnp.maximum(nl, 0.0) + jnp.log1p(jnp.exp(-jnp.abs(nl)))
    half_c = (-0.5 * RG_C) * softplus
    xb = xc.astype(BF16)
    bg = bg_ref[n]
    w = xc.shape[1]
    for hb in range(HALVES):
        cols = slice(hb * MXU_TILE, (hb + 1) * MXU_TILE)
        icols = slice(w + hb * MXU_TILE, w + (hb + 1) * MXU_TILE)
        xh = xb[:, cols]
        gr = jnp.dot(xh, wg_ref[n, hb], preferred_element_type=F32) + bg[:, cols]
        gi = jnp.dot(xh, wg_ref[n, HALVES + hb], preferred_element_type=F32) + bg[:, icols]
        log_a = half_c[:, cols] * _tanh_gate(gr)
        a = jnp.exp(log_a)
        s = -jnp.tanh(log_a) * (1.0 + a * a)
        root = jnp.where(s > 0.0, s * lax.rsqrt(s), 0.0)
        a_scr[n, :, cols] = a
        b_scr[n, :, cols] = root * (_tanh_gate(gi) * (0.5 * xc[:, cols]))


def _tile_dot(lhs, w_ref):
    return jnp.concatenate([jnp.dot(lhs, w_ref[t], preferred_element_type=F32)
                            for t in range(w_ref.shape[0])], axis=1)


def _rglru_body(ta, xf_ref, xfp_ref, xb_ref, xbn_ref, sh_ref, sc_ref, g_ref, wv_ref, wd_ref,
                aconv_ref, wg_ref, bg_ref, lam_ref, h0_ref,
                yf_ref, yb_ref, dv_ref, hfin_ref, st_ref, a_scr, b_scr):
    j = pl.program_id(0)
    rows = ta * SUBLANES
    halo = A_HALO_TOK * SUBLANES
    w = BRANCH_W

    @pl.when(j == 0)
    def _():
        st_ref[...] = h0_ref[...]

    g = g_ref[...]
    sc = sc_ref[...]
    sh = sh_ref[...]
    inner = j > 0

    hf = _norm_mod(jnp.concatenate([xfp_ref[...], xf_ref[...]], axis=0), g, sc, sh).astype(BF16)
    va = _tile_dot(hf, wv_ref)
    dv_ref[...] = _tile_dot(hf[halo:], wd_ref)
    va = jnp.concatenate([jnp.where(inner, va[:halo], 0.0), va[halo:]], axis=0)
    xc = None
    for k in range(A_CONV):
        off = (A_HALO_TOK - (A_CONV - 1) + k) * SUBLANES
        term = aconv_ref[0, k:k + 1, :] * va[off:off + rows]
        xc = term if xc is None else xc + term
    _rglru_coeffs(xc, wg_ref, bg_ref, lam_ref, 0, a_scr, b_scr)

    hb = _norm_mod(jnp.concatenate([xb_ref[...], xbn_ref[...]], axis=0), g, sc, sh).astype(BF16)
    vb = _tile_dot(hb, wv_ref)
    vb = jnp.concatenate([vb[:rows], jnp.where(inner, vb[rows:], 0.0)], axis=0)
    xc = None
    for k in range(A_CONV):
        off = k * SUBLANES
        term = aconv_ref[1, k:k + 1, :] * vb[off:off + rows]
        xc = term if xc is None else xc + term
    _rglru_coeffs(xc, wg_ref, bg_ref, lam_ref, 1, a_scr, b_scr)

    def step(i, carry):
        h_f, h_b = carry
        rf = pl.multiple_of(i * SUBLANES, SUBLANES)
        rb = pl.multiple_of((ta - 1 - i) * SUBLANES, SUBLANES)
        h_f = a_scr[0, pl.ds(rf, SUBLANES), :] * h_f + b_scr[0, pl.ds(rf, SUBLANES), :]
        yf_ref[pl.ds(rf, SUBLANES), :] = h_f
        h_b = a_scr[1, pl.ds(rb, SUBLANES), :] * h_b + b_scr[1, pl.ds(rb, SUBLANES), :]
        yb_ref[pl.ds(rb, SUBLANES), :] = h_b
        return h_f, h_b

    h_f, h_b = lax.fori_loop(0, ta, step, (st_ref[0], st_ref[1]), unroll=8)
    st_ref[0] = h_f
    st_ref[1] = h_b
    hfin_ref[0] = h_f
    hfin_ref[1] = h_b


def _rglru_call(xt, mod, stream, l, p, h0):
    n_rows, d = xt.shape
    rows = TOK * SUBLANES
    halo = A_HALO_TOK * SUBLANES
    n = n_rows // rows
    per = rows // halo
    last_halo = n_rows // halo - 1
    w = BRANCH_W
    in_specs = [
        pl.BlockSpec((rows, d), lambda j: (j, 0)),
        pl.BlockSpec((halo, d), lambda j: (jnp.maximum(j * per - 1, 0), 0)),
        pl.BlockSpec((rows, d), lambda j: (n - 1 - j, 0)),
        pl.BlockSpec((halo, d), lambda j: (jnp.minimum((n - j) * per, last_halo), 0)),
        _mod_spec(mod, l, stream, 0), _mod_spec(mod, l, stream, 1),
        _layer_spec(p["g"], l), _tile_spec(p["wt"], l, WV_TILE0, HALVES),
        _tile_spec(p["wt"], l, WD_TILE0, HALVES), _layer_spec(p["aconv"], l),
        _layer_spec(p["wg"], l), _layer_spec(p["bg"], l), _layer_spec(p["lam"], l),
        pl.BlockSpec((2, SUBLANES, w), lambda j: (0, 0, 0)),
    ]
    out_specs = [
        pl.BlockSpec((rows, w), lambda j: (j, 0)),
        pl.BlockSpec((rows, w), lambda j: (n - 1 - j, 0)),
        pl.BlockSpec((rows, w), lambda j: (j, 0)),
        pl.BlockSpec((2, SUBLANES, w), lambda j: (0, 0, 0)),
    ]
    out_shape = [jax.ShapeDtypeStruct((n_rows, w), F32)] * 3 + [
        jax.ShapeDtypeStruct((2, SUBLANES, w), F32)]
    return pl.pallas_call(
        functools.partial(_rglru_body, TOK),
        grid=(n,),
        in_specs=in_specs,
        out_specs=out_specs,
        out_shape=out_shape,
        scratch_shapes=[pltpu.VMEM((2, SUBLANES, w), F32),
                        pltpu.VMEM((2, rows, w), F32),
                        pltpu.VMEM((2, rows, w), F32)],
        compiler_params=pltpu.CompilerParams(
            dimension_semantics=("arbitrary",), vmem_limit_bytes=VMEM_LIMIT),
        name="rglru",
    )(xt, xt, xt, xt, mod, mod, p["g"], p["wt"], p["wt"], p["aconv"], p["wg"], p["bg"], p["lam"],
      h0)


def _window_sums(padded, w):
    f = padded
    k = 1
    while k < w:
        f = f[:-k] + f[k:]
        k *= 2
    return f


def _window_count(pos, half, size):
    return jnp.minimum(pos + half, size) - jnp.maximum(pos - half, 0)


def _pool_grid_one(v_ref, o_ref, p_scr, w):
    grid_rows = v_ref.shape[0]
    half = w // 2
    tile = v_ref.shape[2:]
    zpad = jnp.zeros((half,) + tile, F32)
    p_scr[0:half] = jnp.zeros((half, GRID_W) + tile, F32)
    p_scr[half + grid_rows:2 * half + grid_rows] = jnp.zeros((half, GRID_W) + tile, F32)

    def row_body(r, carry):
        f = _window_sums(jnp.concatenate([zpad, v_ref[r], zpad], axis=0), w)
        p_scr[half + r] = f[:GRID_W]
        return carry

    lax.fori_loop(0, grid_rows, row_body, 0)

    rpos = lax.broadcasted_iota(jnp.int32, (grid_rows,) + tile, 0)
    rcnt = _window_count(rpos, half, grid_rows)

    def col_body(c, carry):
        f = _window_sums(p_scr[pl.ds(0, grid_rows + 2 * half), c], w)
        cnt = (rcnt * _window_count(c, half, GRID_W)).astype(F32)
        o_ref[:, c] = f[:grid_rows] / cnt - v_ref[:, c]
        return carry

    lax.fori_loop(0, GRID_W, col_body, 0)


def _pool_grid_body(v_ref, o_ref, p_scr):
    gidx = pl.program_id(0)
    for gi, w in enumerate(POOL_WINDOWS):
        @pl.when(gidx == gi)
        def _(w=w):
            _pool_grid_one(v_ref, o_ref, p_scr, w)


def _pool_grid_call(dv):
    n_rows, w = dv.shape
    grid_rows = n_rows // (SUBLANES * GRID_W)
    v4 = dv.reshape(grid_rows, GRID_W, SUBLANES, w)
    blk = (grid_rows, GRID_W, SUBLANES, LANES)
    spec = pl.BlockSpec(blk, lambda gidx: (0, 0, 0, gidx))
    out = pl.pallas_call(
        _pool_grid_body,
        grid=(len(POOL_WINDOWS),),
        in_specs=[spec],
        out_specs=spec,
        out_shape=jax.ShapeDtypeStruct(v4.shape, F32),
        scratch_shapes=[pltpu.VMEM((grid_rows + max(POOL_WINDOWS), GRID_W, SUBLANES, LANES), F32)],
        compiler_params=pltpu.CompilerParams(
            dimension_semantics=("arbitrary",), vmem_limit_bytes=VMEM_LIMIT),
        name="pool_grid",
    )(v4)
    return out.reshape(n_rows, w)


def _pool_seq_body(v_ref, o_ref):
    gidx = pl.program_id(0)
    t = v_ref.shape[0]
    tile = v_ref.shape[1:]
    for gi, w in enumerate(POOL_WINDOWS):
        @pl.when(gidx == gi)
        def _(w=w):
            half = w // 2
            v = v_ref[...]
            zpad = jnp.zeros((half,) + tile, F32)
            f = _window_sums(jnp.concatenate([zpad, v, zpad], axis=0), w)
            pos = lax.broadcasted_iota(jnp.int32, (t,) + tile, 0)
            cnt = _window_count(pos, half, t).astype(F32)
            o_ref[...] = f[:t] / cnt - v


def _pool_seq_call(dv):
    n_rows, w = dv.shape
    t = n_rows // SUBLANES
    v3 = dv.reshape(t, SUBLANES, w)
    spec = pl.BlockSpec((t, SUBLANES, LANES), lambda gidx: (0, 0, gidx))
    out = pl.pallas_call(
        _pool_seq_body,
        grid=(len(POOL_WINDOWS),),
        in_specs=[spec],
        out_specs=spec,
        out_shape=jax.ShapeDtypeStruct(v3.shape, F32),
        compiler_params=pltpu.CompilerParams(
            dimension_semantics=("arbitrary",), vmem_limit_bytes=VMEM_LIMIT),
        name="pool_seq",
    )(v3)
    return out.reshape(n_rows, w)


CONV_FIRST = C_HALO_TOK - C_CONV // 2


def _conv31_block(cb_ref, cconv_ref, cv_ref, lb, row0):
    accs = [None] * CONV_TOK_BLOCK
    taps = {}
    for m in range(CONV_TOK_BLOCK + C_CONV - 1):
        tile = cb_ref[lb, pl.ds(row0 + (CONV_FIRST + m) * SUBLANES, SUBLANES), :]
        for q in range(CONV_TOK_BLOCK):
            k = m - q
            if 0 <= k < C_CONV:
                if k not in taps:
                    taps[k] = cconv_ref[lb, k]
                term = taps[k] * tile
                accs[q] = term if accs[q] is None else accs[q] + term
        taps.pop(m - CONV_TOK_BLOCK + 1, None)
    for q in range(CONV_TOK_BLOCK):
        cv_ref[lb, pl.ds(row0 + q * SUBLANES, SUBLANES), :] = accs[q]


def _conv31_head_fixup(head, cconv_ref, cv_ref, tb):
    reach = C_CONV // 2
    for lb in range(LANE_BLOCKS):
        lanes = slice(lb * LANES, (lb + 1) * LANES)
        for i in range(reach):
            pos = tb - reach + i
            add = None
            for k in range(C_CONV - 1 - i, C_CONV):
                t = k - (C_CONV - 1 - i)
                term = cconv_ref[lb, k] * head[t * SUBLANES:(t + 1) * SUBLANES, lanes]
                add = term if add is None else add + term
            r0 = pos * SUBLANES
            cv_ref[lb, r0:r0 + SUBLANES, :] = cv_ref[lb, r0:r0 + SUBLANES, :] + add


def _mix_body(tb, n, final_norm, xp_ref, xr_ref, sh_ref, sc_ref, gt_ref, g_ref, wt_ref,
              bconv_ref, cconv_ref, lng_ref, lnb_ref, cpw_ref, cpwb_ref, dw_ref, db_ref, dsc_ref,
              wout_ref, yf_ref, yb_ref, p_ref, fg_ref, o_ref,
              hnew_scr, projc_scr, projz_scr, curb_scr, tailb_scr, cbb_scr,
              curc_scr, tailc_scr, cbc_scr, cv_scr):
    j = pl.program_id(0)
    rows = tb * SUBLANES
    hb = B_HALO_TOK * SUBLANES
    hc = C_HALO_TOK * SUBLANES
    d = xp_ref.shape[1]
    gen = j % 2
    new_z = projz_scr.at[gen]
    old_z = projz_scr.at[1 - gen]
    blocks_per_lane = tb // CONV_TOK_BLOCK
    blocks_per_tile = LANE_BLOCKS * blocks_per_lane // (N_CONV_TILES + N_GATE_TILES - HALVES)

    @pl.when(j == 0)
    def _():
        projz_scr[1] = jnp.zeros(projz_scr.shape[1:], F32)
        curb_scr[...] = jnp.zeros(curb_scr.shape, F32)
        tailb_scr[...] = jnp.zeros(tailb_scr.shape, F32)
        curc_scr[...] = jnp.zeros(curc_scr.shape, F32)
        tailc_scr[...] = jnp.zeros(tailc_scr.shape, F32)

    hnew_scr[...] = _norm_mod(xp_ref[...], g_ref[...], sc_ref[...], sh_ref[...]).astype(BF16)

    def slot(s, h):
        return old_z[s * HALVES + h]

    def wout_rows(branch, h):
        r0 = branch * BRANCH_W + h * MXU_TILE
        return wout_ref[r0:r0 + MXU_TILE, :]

    def half(v, h):
        return v[:, h * MXU_TILE:(h + 1) * MXU_TILE]

    cbc_scr[:, 0:hc, :] = tailc_scr[...]
    cbc_scr[:, hc:hc + rows, :] = curc_scr[...]
    cbc_scr[:, hc + rows:, :] = jnp.zeros((LANE_BLOCKS, hc, LANES), F32)

    def conv_blocks(first_block, i):
        for r in range(blocks_per_tile):
            blk = first_block + i * blocks_per_tile + r
            lb = blk // blocks_per_lane
            row0 = pl.multiple_of((blk % blocks_per_lane) * (CONV_TOK_BLOCK * SUBLANES),
                                  CONV_TOK_BLOCK * SUBLANES)
            _conv31_block(cbc_scr, cconv_ref, cv_scr, lb, row0)

    def conv_tile(i, carry):
        projc_scr[i] = jnp.dot(hnew_scr[...], wt_ref[i], preferred_element_type=F32)
        conv_blocks(0, i)
        return carry

    lax.fori_loop(0, N_CONV_TILES, conv_tile, 0, unroll=TILE_UNROLL)

    def gate_tile(i, carry):
        new_z[i] = jnp.dot(hnew_scr[...], wt_ref[N_CONV_TILES + i], preferred_element_type=F32)
        conv_blocks(N_CONV_TILES * blocks_per_tile, i)
        return carry

    lax.fori_loop(0, N_GATE_TILES - HALVES, gate_tile, 0, unroll=TILE_UNROLL)
    for i in range(N_GATE_TILES - HALVES, N_GATE_TILES):
        new_z[i] = jnp.dot(hnew_scr[...], wt_ref[N_CONV_TILES + i], preferred_element_type=F32)

    has_next = j < n
    ub_new = jnp.concatenate([projc_scr[h] * projc_scr[HALVES + h] for h in range(HALVES)], axis=1)
    uc_new = jnp.concatenate(
        [(0.5 * projc_scr[2 * HALVES + h]) * _tanh_gate(projc_scr[3 * HALVES + h])
         for h in range(HALVES)], axis=1)
    _conv31_head_fixup(jnp.where(has_next, uc_new[0:hc], 0.0), cconv_ref, cv_scr, tb)
    cbb_scr[0:hb] = tailb_scr[...]
    cbb_scr[hb:hb + rows] = curb_scr[...]
    cbb_scr[hb + rows:] = jnp.where(has_next, ub_new[0:hb], 0.0)
    tailb_scr[...] = curb_scr[rows - hb:rows]
    curb_scr[...] = ub_new
    tailc_scr[...] = curc_scr[:, rows - hc:rows, :]
    for lb in range(LANE_BLOCKS):
        curc_scr[lb] = uc_new[:, lb * LANES:(lb + 1) * LANES]

    conv_b = None
    for k in range(B_CONV):
        term = bconv_ref[k:k + 1, :] * cbb_scr[k * SUBLANES:k * SUBLANES + rows, :]
        conv_b = term if conv_b is None else conv_b + term

    cv = jnp.concatenate([cv_scr[lb] for lb in range(LANE_BLOCKS)], axis=1)
    mu = jnp.mean(cv, axis=-1, keepdims=True)
    cen = cv - mu
    var = jnp.mean(cen * cen, axis=-1, keepdims=True)
    ln = cen * lax.rsqrt(var + LN_EPS) * lng_ref[...] + lnb_ref[...]
    yc = jnp.dot(_silu(ln).astype(BF16), cpw_ref[...], preferred_element_type=F32) + cpwb_ref[...]

    ya = yf_ref[...] + yb_ref[...]
    pooled = p_ref[...].astype(BF16)
    out = None
    for h in range(HALVES):
        za, bb, zb, zc, zd = (slot(s, h) for s in range(len(MIX_GATE_SLOTS)))
        yd = (jnp.dot(half(pooled, h), dw_ref[h], preferred_element_type=F32)
              + half(db_ref[...], h)) * half(dsc_ref[...], h)
        cats = (half(ya, h) * (za * _tanh_gate(za)),
                bb * half(conv_b, h) * (zb * _tanh_gate(zb)),
                half(yc, h) * (zc * _tanh_gate(zc)),
                yd * (zd * _tanh_gate(zd)))
        for branch, cat in enumerate(cats):
            term = jnp.dot(cat.astype(BF16), wout_rows(branch, h), preferred_element_type=F32)
            out = term if out is None else out + term

    res = xr_ref[...].reshape(tb, SUBLANES, d) + gt_ref[...][None] * out.reshape(tb, SUBLANES, d)
    res = res.reshape(rows, d)
    if final_norm:
        ms = jnp.mean(res * res, axis=-1, keepdims=True)
        res = res * lax.rsqrt(ms + RMS_EPS) * fg_ref[...]
    o_ref[...] = res


def _mix_call(xt, mod, stream, l, p, yf, yb, pooled, fg, final_norm):
    n_rows, d = xt.shape
    rows = TOK * SUBLANES
    n = n_rows // rows
    w = BRANCH_W
    hb = B_HALO_TOK * SUBLANES
    hc = C_HALO_TOK * SUBLANES
    assert (LANE_BLOCKS * (TOK // CONV_TOK_BLOCK)) % (N_CONV_TILES + N_GATE_TILES - HALVES) == 0
    ahead = lambda j: (jnp.minimum(j, n - 1), 0)
    behind = lambda j: (jnp.maximum(j - 1, 0), 0)
    names = ["g", "wt", "bconv", "cconv", "lng", "lnb", "cpw", "cpwb", "dw", "db", "dsc", "wout"]
    in_specs = (
        [pl.BlockSpec((rows, d), ahead), pl.BlockSpec((rows, d), behind),
         _mod_spec(mod, l, stream, 0), _mod_spec(mod, l, stream, 1), _mod_spec(mod, l, stream, 2)]
        + [_tile_spec(p[k], l, 0, WV_TILE0) if k == "wt" else _layer_spec(p[k], l) for k in names]
        + [pl.BlockSpec((rows, w), behind)] * 3
        + [pl.BlockSpec(fg.shape, lambda j: (0, 0))])
    return pl.pallas_call(
        functools.partial(_mix_body, TOK, n, final_norm),
        grid=(n + 1,),
        in_specs=in_specs,
        out_specs=pl.BlockSpec((rows, d), behind),
        out_shape=jax.ShapeDtypeStruct((n_rows, d), F32),
        scratch_shapes=[pltpu.VMEM((rows, d), BF16),
                        pltpu.VMEM((N_CONV_TILES, rows, MXU_TILE), F32),
                        pltpu.VMEM((2, N_GATE_TILES, rows, MXU_TILE), F32),
                        pltpu.VMEM((rows, w), F32), pltpu.VMEM((hb, w), F32),
                        pltpu.VMEM((rows + 2 * hb, w), F32),
                        pltpu.VMEM((LANE_BLOCKS, rows, LANES), F32),
                        pltpu.VMEM((LANE_BLOCKS, hc, LANES), F32),
                        pltpu.VMEM((LANE_BLOCKS, rows + 2 * hc, LANES), F32),
                        pltpu.VMEM((LANE_BLOCKS, rows, LANES), F32)],
        compiler_params=pltpu.CompilerParams(
            dimension_semantics=("arbitrary",), vmem_limit_bytes=VMEM_LIMIT),
        name="mix",
    )(xt, xt, mod, mod, mod, *[p[k] for k in names], yf, yb, pooled, fg)


def _block_diag(blocks):
    gcount, i, jdim = blocks.shape[-3:]
    eye = jnp.eye(gcount, dtype=blocks.dtype)
    dense = eye[:, None, :, None] * blocks[..., :, :, None, :]
    return dense.reshape(blocks.shape[:-3] + (gcount * i, gcount * jdim))


def _diag_tiles(blocks):
    gcount, i = blocks.shape[-3:-1]
    per_tile = MXU_TILE // i
    grouped = blocks.reshape(blocks.shape[:-3] + (gcount // per_tile, per_tile, i, i))
    return _block_diag(grouped)


def _time_major(a):
    b, t, d = a.shape
    return a.transpose(1, 0, 2).reshape(t * b, d)


def _weight_tiles_body(src_ref, halve_ref, w_ref, o_ref):
    factor = jnp.where(halve_ref[pl.program_id(1)] == 1, 0.5, 1.0)
    for h in range(HALVES):
        o_ref[h] = (w_ref[:, h * MXU_TILE:(h + 1) * MXU_TILE] * factor).astype(BF16)


def _weight_tiles(w_in):
    depth, d, _ = w_in.shape
    src = list(TILE_SLOTS)
    halve = [int(s in HALVED_SLOTS) for s in TILE_SLOTS]
    grid_spec = pltpu.PrefetchScalarGridSpec(
        num_scalar_prefetch=2,
        grid=(depth, len(src)),
        in_specs=[pl.BlockSpec((None, d, BRANCH_W), lambda l, t, src_ref, halve_ref: (l, 0, src_ref[t]))],
        out_specs=pl.BlockSpec((None, HALVES, d, MXU_TILE),
                               lambda l, t, src_ref, halve_ref: (l, t, 0, 0)))
    return pl.pallas_call(
        _weight_tiles_body,
        grid_spec=grid_spec,
        out_shape=jax.ShapeDtypeStruct((depth, len(src) * HALVES, d, MXU_TILE), BF16),
        compiler_params=pltpu.CompilerParams(
            dimension_semantics=("arbitrary", "arbitrary"), vmem_limit_bytes=VMEM_LIMIT),
        name="weight_tiles",
    )(jnp.asarray(src, jnp.int32), jnp.asarray(halve, jnp.int32), w_in)


def _prepare_params(norm_g, w_in, w_out, a_conv, a_wr, a_br, a_wi, a_bi, a_lam, b_conv, c_conv,
                    c_ln_g, c_ln_b, c_pw, c_pw_b, d_w, d_b, d_scale):
    depth, d, _ = w_in.shape
    w = BRANCH_W
    gates = jnp.stack([a_wr, a_wi], axis=2)
    wg = (0.5 * _diag_tiles(gates)).reshape(depth, 2, 2 * HALVES, MXU_TILE, MXU_TILE).astype(BF16)
    cconv = c_conv.reshape(depth, C_CONV, LANE_BLOCKS, 1, LANES).transpose(0, 2, 1, 3, 4)
    return dict(
        g=norm_g.reshape(depth, 1, d),
        wt=_weight_tiles(w_in),
        aconv=a_conv, wg=wg,
        bg=(0.5 * jnp.concatenate([a_br, a_bi], axis=-1)).reshape(depth, 2, 1, 2 * w),
        lam=a_lam.reshape(depth, 2, 1, w),
        bconv=b_conv,
        cconv=jnp.broadcast_to(cconv, (depth, LANE_BLOCKS, C_CONV, SUBLANES, LANES)),
        lng=c_ln_g.reshape(depth, 1, w), lnb=c_ln_b.reshape(depth, 1, w),
        cpw=c_pw.astype(BF16), cpwb=c_pw_b.reshape(depth, 1, w),
        dw=_diag_tiles(d_w).astype(BF16), db=d_b.reshape(depth, 1, w),
        dsc=d_scale.reshape(depth, 1, w),
        wout=w_out.astype(BF16))


def kernel(x, c, ctx, c_ctx, mod_w, mod_b, norm_g, w_in, w_out, a_conv, a_wr, a_br, a_wi, a_bi,
           a_lam, b_conv, c_conv, c_ln_g, c_ln_b, c_pw, c_pw_b, d_w, d_b, d_scale, final_g):
    bsz, seq, d = x.shape
    ctx_len = ctx.shape[1]
    depth = w_in.shape[0]
    w = BRANCH_W
    assert bsz == SUBLANES and d == 2 * w
    assert seq % GRID_W == 0 and seq % TOK == 0 and ctx_len % TOK == 0

    xt = _time_major(x)
    ct = _time_major(ctx)
    cc = jnp.concatenate([c, jnp.broadcast_to(c_ctx[None, :], (SUBLANES, d))], axis=0)
    mod = _modulation(cc, mod_w, mod_b).reshape(depth, 2, SUBLANES, 3 * d)
    p = _prepare_params(norm_g, w_in, w_out, a_conv, a_wr, a_br, a_wi, a_bi, a_lam, b_conv, c_conv,
                        c_ln_g, c_ln_b, c_pw, c_pw_b, d_w, d_b, d_scale)
    fg = final_g.reshape(1, d)
    zeros_state = jnp.zeros((2, SUBLANES, w), F32)

    for l in range(depth):
        last = l == depth - 1
        yf, yb, dv, state_k = _rglru_call(ct, mod, 1, l, p, zeros_state)
        if not last:
            ct = _mix_call(ct, mod, 1, l, p, yf, yb, _pool_seq_call(dv), fg, False)
        yf, yb, dv, _ = _rglru_call(xt, mod, 0, l, p, state_k)
        xt = _mix_call(xt, mod, 0, l, p, yf, yb, _pool_grid_call(dv), fg, last)

    return xt.reshape(seq, bsz, d).transpose(1, 0, 2)
```
